```python
import jax, jax.numpy as jnp
from jax import lax
import numpy as np

D_MODEL = 1024
BATCH = 2
SEQ = 16384
DEPTH = 4

HEAD_DIM = 128
DILATED_GROUPS = ((128, 1), (512, 4), (2048, 16))
HEADS_PER_GROUP = 4
N_ATTN_HEADS = HEADS_PER_GROUP * len(DILATED_GROUPS)
ATTN_WIDTH = N_ATTN_HEADS * HEAD_DIM
ATTN_OUT = HEADS_PER_GROUP * HEAD_DIM
BLOCK = 128
ROPE_THETA = 500000.0
ROT_DIM = HEAD_DIM // 4
D_CONV = D_MODEL
CONV_K = 31
D_FF = 256 * ((8 * D_MODEL // 3 + 255) // 256)
FFN_K = 3
IN_WIDTH = 3 * ATTN_WIDTH + 2 * D_CONV + 2 * D_MODEL
EPS = 1e-6

kernel_name = "hybrid_dilated_attn_conformer_convffn_adaln"


def rms_norm(x, g):
    x32 = x.astype(jnp.float32)
    y = x32 * lax.rsqrt(jnp.mean(x32 * x32, axis=-1, keepdims=True) + EPS)
    return (y * g.astype(jnp.float32)).astype(x.dtype)


def layer_norm(x, g, b):
    x32 = x.astype(jnp.float32)
    mu = jnp.mean(x32, axis=-1, keepdims=True)
    xc = x32 - mu
    y = xc * lax.rsqrt(jnp.mean(xc * xc, axis=-1, keepdims=True) + EPS)
    return (y * g.astype(jnp.float32) + b.astype(jnp.float32)).astype(x.dtype)


def causal_dwconv(x, w, b):
    K, C = w.shape
    y = lax.conv_general_dilated(x, w[:, None, :].astype(x.dtype), window_strides=(1,),
                                 padding=[(K - 1, 0)], dimension_numbers=("NWC", "WIO", "NWC"),
                                 feature_group_count=C)
    return y + b.astype(x.dtype)


def partial_rope(t, cos, sin):
    half = ROT_DIM // 2
    t1, t2 = t[..., :half], t[..., half:ROT_DIM]
    return jnp.concatenate([t1 * cos - t2 * sin, t2 * cos + t1 * sin, t[..., ROT_DIM:]], axis=-1)


def dilated_band_attention(q, k, v, window, dilation):
    B, S, H, Dh = q.shape
    band = window // dilation
    sub_len = -(-S // (dilation * BLOCK)) * BLOCK
    pad = sub_len * dilation - S
    nb = sub_len // BLOCK

    def to_sub(t):
        t = jnp.pad(t, ((0, 0), (0, pad), (0, 0), (0, 0)))
        t = t.reshape(B, sub_len, dilation, H, Dh).transpose(0, 2, 3, 1, 4)
        return t.reshape(B, dilation, H, nb, BLOCK, Dh)

    def with_prev(t):
        prev = jnp.pad(t[:, :, :, :-1], ((0, 0), (0, 0), (0, 0), (1, 0), (0, 0), (0, 0)))
        return jnp.concatenate([prev, t], axis=4)

    qb = to_sub(q)
    kw = with_prev(to_sub(k))
    vw = with_prev(to_sub(v))
    s = jnp.einsum('brhnqd,brhnkd->brhnqk', qb, kw) * (Dh ** -0.5)
    qi = jnp.arange(BLOCK)[:, None]
    kj = jnp.arange(2 * BLOCK)[None, :]
    dist = qi + BLOCK - kj
    valid = (dist >= 0) & (dist <= band)
    not_first = jnp.arange(nb)[:, None, None] > 0
    valid = valid[None] & (not_first | (kj >= BLOCK)[None])
    s = jnp.where(valid, s, -jnp.inf)
    m = jnp.max(s, axis=-1, keepdims=True)
    p = jnp.exp(s - m)
    l = jnp.sum(p, axis=-1, keepdims=True)
    o = jnp.einsum('brhnqk,brhnkd->brhnqd', p, vw) / l
    lse = (m + jnp.log(l))[..., 0]
    o = o.reshape(B, dilation, H, sub_len, Dh).transpose(0, 3, 1, 2, 4)
    o = o.reshape(B, sub_len * dilation, H, Dh)[:, :S]
    lse = lse.reshape(B, dilation, H, sub_len).transpose(0, 3, 1, 2)
    lse = lse.reshape(B, sub_len * dilation, H)[:, :S]
    return o, lse


def setup_inputs(seed: int = 0) -> dict:
    key = jax.random.key(seed)
    ks = jax.random.split(key, 24)
    f32 = jnp.float32

    def nrm(k, shape, fan_in, s=1.0):
        return jax.random.normal(k, shape, f32) * (s * fan_in ** -0.5)

    def gain(k, shape):
        return 1.0 + 0.05 * jax.random.normal(k, shape, f32)

    def bias(k, shape):
        return 0.02 * jax.random.normal(k, shape, f32)

    return {
        "x": jax.random.normal(ks[0], (BATCH, SEQ, D_MODEL), f32),
        "c": jax.random.normal(ks[1], (BATCH, D_MODEL), f32),
        "positions": jnp.broadcast_to(jnp.arange(SEQ, dtype=jnp.int32), (BATCH, SEQ)),
        "w_ada": nrm(ks[2], (DEPTH, D_MODEL, 6 * D_MODEL), D_MODEL, 0.5),
        "b_ada": bias(ks[3], (DEPTH, 6 * D_MODEL)),
        "g_norm1": gain(ks[4], (DEPTH, D_MODEL)),
        "w_in": nrm(ks[5], (DEPTH, D_MODEL, IN_WIDTH), D_MODEL),
        "g_q": gain(ks[6], (DEPTH, HEAD_DIM)),
        "g_k": gain(ks[7], (DEPTH, HEAD_DIM)),
        "w_attn_proj": nrm(ks[8], (DEPTH, ATTN_OUT, D_MODEL), ATTN_OUT),
        "w_conv_dw": nrm(ks[9], (DEPTH, CONV_K, D_CONV), CONV_K),
        "b_conv_dw": bias(ks[10], (DEPTH, D_CONV)),
        "g_conv_ln": gain(ks[11], (DEPTH, D_CONV)),
        "b_conv_ln": bias(ks[12], (DEPTH, D_CONV)),
        "w_conv_out": nrm(ks[13], (DEPTH, D_CONV, D_MODEL), D_CONV),
        "w_o": nrm(ks[14], (DEPTH, D_MODEL, D_MODEL), D_MODEL),
        "g_norm2": gain(ks[15], (DEPTH, D_MODEL)),
        "w_ffn_in": nrm(ks[16], (DEPTH, D_MODEL, 2 * D_FF), D_MODEL),
        "w_ffn_dw": nrm(ks[17], (DEPTH, FFN_K, D_FF), FFN_K),
        "b_ffn_dw": bias(ks[18], (DEPTH, D_FF)),
        "w_ffn_down": nrm(ks[19], (DEPTH, D_FF, D_MODEL), D_FF),
    }


def reference(x, c, positions, w_ada, b_ada, g_norm1, w_in, g_q, g_k, w_attn_proj,
              w_conv_dw, b_conv_dw, g_conv_ln, b_conv_ln, w_conv_out, w_o, g_norm2,
              w_ffn_in, w_ffn_dw, b_ffn_dw, w_ffn_down):
    f32 = jnp.float32
    B, S, _ = x.shape
    inv_freq = ROPE_THETA ** (-jnp.arange(0, ROT_DIM, 2, dtype=f32) / ROT_DIM)
    ang = positions.astype(f32)[..., None] * inv_freq
    cos = jnp.cos(ang)[:, :, None, :]
    sin = jnp.sin(ang)[:, :, None, :]
    c_act = jax.nn.silu(c)
    split_at = [ATTN_WIDTH, 2 * ATTN_WIDTH, 3 * ATTN_WIDTH,
                3 * ATTN_WIDTH + D_CONV, 3 * ATTN_WIDTH + 2 * D_CONV,
                3 * ATTN_WIDTH + 2 * D_CONV + D_MODEL]

    for l in range(DEPTH):
        mod = (c_act @ w_ada[l] + b_ada[l])[:, None, :]
        sh1, sc1, gt1, sh2, sc2, gt2 = jnp.split(mod, 6, axis=-1)

        h = rms_norm(x, g_norm1[l]) * (1.0 + sc1) + sh1
        z = h @ w_in[l]
        q, k, v, c_val, c_gate, gate_a, gate_b = jnp.split(z, split_at, axis=-1)

        q = partial_rope(rms_norm(q.astype(f32).reshape(B, S, N_ATTN_HEADS, HEAD_DIM), g_q[l]), cos, sin)
        k = partial_rope(rms_norm(k.astype(f32).reshape(B, S, N_ATTN_HEADS, HEAD_DIM), g_k[l]), cos, sin)
        v = v.astype(f32).reshape(B, S, N_ATTN_HEADS, HEAD_DIM)
        outs, lses = [], []
        for gi, (win, dil) in enumerate(DILATED_GROUPS):
            hs = slice(gi * HEADS_PER_GROUP, (gi + 1) * HEADS_PER_GROUP)
            o_g, lse_g = dilated_band_attention(q[:, :, hs], k[:, :, hs], v[:, :, hs], win, dil)
            outs.append(o_g)
            lses.append(lse_g)
        wts = jax.nn.softmax(jnp.stack(lses, axis=0), axis=0)
        attn = jnp.sum(wts[..., None] * jnp.stack(outs, axis=0), axis=0)
        y_a = attn.reshape(B, S, ATTN_OUT).astype(x.dtype) @ w_attn_proj[l]

        u = c_val * jax.nn.sigmoid(c_gate)
        u = causal_dwconv(u, w_conv_dw[l], b_conv_dw[l])
        u = jax.nn.silu(layer_norm(u, g_conv_ln[l], b_conv_ln[l]))
        y_b = u @ w_conv_out[l]

        merged = jax.nn.sigmoid(gate_a) * y_a + jax.nn.sigmoid(gate_b) * y_b
        x = x + gt1 * (merged @ w_o[l])

        h2 = rms_norm(x, g_norm2[l]) * (1.0 + sc2) + sh2
        gu = h2 @ w_ffn_in[l]
        g_path, u_path = jnp.split(gu, 2, axis=-1)
        g_path = causal_dwconv(g_path, w_ffn_dw[l], b_ffn_dw[l])
        x = x + gt2 * ((jax.nn.silu(g_path) * u_path) @ w_ffn_down[l])

    return x
```

```python
import functools

import jax
import jax.numpy as jnp
from jax import lax
from jax.experimental import pallas as pl
from jax.experimental.pallas import tpu as pltpu

F32 = jnp.float32
BF16 = jnp.bfloat16

D_MODEL = 1024
HEAD_DIM = 128
DILATIONS = (1, 4, 16)
HEADS_PER_GROUP = 4
N_HEADS = HEADS_PER_GROUP * len(DILATIONS)
ATTN_WIDTH = N_HEADS * HEAD_DIM
ATTN_OUT = HEADS_PER_GROUP * HEAD_DIM
BLOCK = 128
ROT_DIM = HEAD_DIM // 4
ROT_HALF = ROT_DIM // 2
ROPE_THETA = 500000.0
CONV_K = 31
D_FF = 2816
FFN_K = 3
IN_WIDTH = 3 * ATTN_WIDTH + 4 * D_MODEL
EPS = 1e-6
NEG = -1e30

TM_IN = 512
TILE_ATTN = 2048
TM_MIX = 512
TM_FFN = 512
CONV_HALO = 32
FFN_HALO = 8
V7X_VMEM_LIMIT = 56 * 1024 * 1024


def _params(n_axes):
    return pltpu.CompilerParams(dimension_semantics=("arbitrary",) * n_axes,
                                vmem_limit_bytes=V7X_VMEM_LIMIT)


def _sigmoid(x):
    return 1.0 / (1.0 + jnp.exp(-x))


def _ada_kernel(c_ref, w_ref, b_ref, o_ref):
    c = c_ref[...]
    c_act = c * _sigmoid(c)
    o_ref[...] = jnp.dot(c_act, w_ref[...], preferred_element_type=F32) + b_ref[...]


def _ada(c, w_ada, b_ada):
    depth, d, n = w_ada.shape
    bsz = c.shape[0]
    nt = 1024
    return pl.pallas_call(
        _ada_kernel,
        grid=(depth, n // nt),
        in_specs=[pl.BlockSpec((bsz, d), lambda l, j: (0, 0)),
                  pl.BlockSpec((None, d, nt), lambda l, j: (l, 0, j)),
                  pl.BlockSpec((None, 1, nt), lambda l, j: (l, 0, j))],
        out_specs=pl.BlockSpec((None, bsz, nt), lambda l, j: (l, 0, j)),
        out_shape=jax.ShapeDtypeStruct((depth, bsz, n), F32),
        compiler_params=_params(2),
        name="ada",
    )(c, w_ada, b_ada.reshape(depth, 1, n))


def _rms_mod(x, g, shift, scale):
    ms = jnp.mean(x * x, axis=-1, keepdims=True)
    h = x * lax.rsqrt(ms + EPS) * g
    return h * (1.0 + scale) + shift


def _in_proj_kernel(x_ref, mod_ref, g1_ref, w_ref, gq_ref, gk_ref, pgq_ref, pgk_ref, cos_ref, sin_ref,
                    q_ref, k_ref, v_ref, u_ref, ga_ref, gb_ref, zs_ref, zs2_ref):
    tm = x_ref.shape[0]
    h = _rms_mod(x_ref[...], g1_ref[...], mod_ref[0:1, :], mod_ref[1:2, :]).astype(BF16)

    cos = cos_ref[...]
    sin = sin_ref[...]
    lane = lax.broadcasted_iota(jnp.int32, (tm, HEAD_DIM), 1)
    low = lane < ROT_HALF

    def store_head(o_ref, col, val, dil, slot):
        cols = slice(col, col + HEAD_DIM)
        if dil == 1:
            o_ref[:, cols] = val.astype(BF16)
            return
        zs = zs_ref.at[slot]
        zs[...] = val
        per = tm // 4
        if dil == 4:
            for r in range(4):
                o_ref[r * per:(r + 1) * per, cols] = zs[pl.ds(r, per, stride=4), :].astype(BF16)
            return
        zs2 = zs2_ref.at[slot]
        for r in range(4):
            zs2[r * per:(r + 1) * per, :] = zs[pl.ds(r, per, stride=4), :]
        per2 = per // 4
        for r in range(4):
            for rp in range(4):
                r16 = r + 4 * rp
                o_ref[r16 * per2:(r16 + 1) * per2, cols] = (
                    zs2[pl.ds(r * per + rp, per2, stride=4), :].astype(BF16))

    def norm_rope(t, a_tab, b_tab, post):
        ms = jnp.mean(t * t, axis=-1, keepdims=True)
        rinv = lax.rsqrt(ms + EPS) * post
        partner = jnp.where(low, pltpu.roll(t, HEAD_DIM - ROT_HALF, 1), pltpu.roll(t, ROT_HALF, 1))
        return (t * a_tab + partner * b_tab) * rinv

    slot = 0
    for which, (o_ref, g_ref, pg_ref, post) in enumerate(
            ((q_ref, gq_ref, pgq_ref, HEAD_DIM ** -0.5), (k_ref, gk_ref, pgk_ref, 1.0), (v_ref, None, None, None))):
        if g_ref is not None:
            a_tab = cos * g_ref[...]
            b_tab = sin * pg_ref[...]
        for gi, dil in enumerate(DILATIONS):
            base = which * ATTN_WIDTH + gi * ATTN_OUT
            z = jnp.dot(h, w_ref[:, base:base + ATTN_OUT], preferred_element_type=F32)
            for hh in range(HEADS_PER_GROUP):
                t = z[:, hh * HEAD_DIM:(hh + 1) * HEAD_DIM]
                if g_ref is not None:
                    t = norm_rope(t, a_tab, b_tab, post)
                store_head(o_ref, gi * ATTN_OUT + hh * HEAD_DIM, t, dil, slot % zs_ref.shape[0])
                slot += 1

    cw = 512
    base_val = 3 * ATTN_WIDTH
    base_gate = base_val + D_MODEL
    for cc in range(D_MODEL // cw):
        val = jnp.dot(h, w_ref[:, base_val + cc * cw:base_val + (cc + 1) * cw], preferred_element_type=F32)
        gate = jnp.dot(h, w_ref[:, base_gate + cc * cw:base_gate + (cc + 1) * cw], preferred_element_type=F32)
        u_ref[:, cc * cw:(cc + 1) * cw] = (val * _sigmoid(gate)).astype(BF16)
    base_a = base_gate + D_MODEL
    base_b = base_a + D_MODEL
    for cc in range(D_MODEL // cw):
        za = jnp.dot(h, w_ref[:, base_a + cc * cw:base_a + (cc + 1) * cw], preferred_element_type=F32)
        ga_ref[:, cc * cw:(cc + 1) * cw] = _sigmoid(za).astype(BF16)
        zb = jnp.dot(h, w_ref[:, base_b + cc * cw:base_b + (cc + 1) * cw], preferred_element_type=F32)
        gb_ref[:, cc * cw:(cc + 1) * cw] = _sigmoid(zb).astype(BF16)


def _in_proj(x, mod, g1, w_in, gq, gk, pgq, pgk, cos_tab, sin_tab, layer):
    bsz, seq, d = x.shape
    tm = TM_IN
    n_slots = 4
    row = lambda b, i: (b, i, 0)
    par = lambda b, i: (layer, 0, 0)
    out_attn = jax.ShapeDtypeStruct((bsz, seq, ATTN_WIDTH), BF16)
    out_d = jax.ShapeDtypeStruct((bsz, seq, d), BF16)
    return pl.pallas_call(
        _in_proj_kernel,
        grid=(bsz, seq // tm),
        in_specs=[pl.BlockSpec((None, tm, d), row),
                  pl.BlockSpec((None, None, 6, d), lambda b, i: (layer, b, 0, 0)),
                  pl.BlockSpec((None, 1, d), par),
                  pl.BlockSpec((None, d, IN_WIDTH), par, pipeline_mode=pl.Buffered(1)),
                  pl.BlockSpec((None, 1, HEAD_DIM), par),
                  pl.BlockSpec((None, 1, HEAD_DIM), par),
                  pl.BlockSpec((None, 1, HEAD_DIM), par),
                  pl.BlockSpec((None, 1, HEAD_DIM), par),
                  pl.BlockSpec((None, tm, HEAD_DIM), row),
                  pl.BlockSpec((None, tm, HEAD_DIM), row)],
        out_specs=[pl.BlockSpec((None, tm, ATTN_WIDTH), row)] * 3 + [pl.BlockSpec((None, tm, d), row)] * 3,
        out_shape=[out_attn] * 3 + [out_d] * 3,
        scratch_shapes=[pltpu.VMEM((n_slots, tm, HEAD_DIM), F32), pltpu.VMEM((n_slots, tm, HEAD_DIM), F32)],
        compiler_params=_params(2),
        name="in_proj",
    )(x, mod, g1, w_in, gq, gk, pgq, pgk, cos_tab, sin_tab)


def _band_block(q, kc, vc, valid):
    s = lax.dot_general(q, kc, (((1,), (1,)), ((), ())), preferred_element_type=F32)
    s = jnp.where(valid, s, NEG)
    m = jnp.max(s, axis=-1, keepdims=True)
    p = jnp.exp(s - m)
    l = jnp.sum(p, axis=-1, keepdims=True)
    o = jnp.dot(p.astype(BF16), vc, preferred_element_type=F32)
    o = o * (1.0 / l)
    lse = jnp.broadcast_to(m + jnp.log(l), (BLOCK, HEAD_DIM))
    return o, lse


def _attn_kernel(q0_ref, q1_ref, q2_ref, k0_ref, k1_ref, k2_ref, v0_ref, v1_ref, v2_ref,
                 k0h_ref, k1h_ref, k2h_ref, v0h_ref, v1h_ref, v2h_ref, o_ref,
                 o0_s, l0_s, o1_s, l1_s, o2_s, l2_s, oa_s, la_s):
    first = pl.program_id(1) == 0
    qi = lax.broadcasted_iota(jnp.int32, (BLOCK, 2 * BLOCK), 0)
    kj = lax.broadcasted_iota(jnp.int32, (BLOCK, 2 * BLOCK), 1)
    band = (kj >= qi) & (kj <= qi + BLOCK)
    band_halo = band & (kj >= jnp.where(first, BLOCK, 0))

    nblk = TILE_ATTN // BLOCK
    cat = functools.partial(jnp.concatenate, axis=0)

    o, lse = _band_block(q0_ref[0:BLOCK, :], cat([k0h_ref[...], k0_ref[0:BLOCK, :]]),
                         cat([v0h_ref[...], v0_ref[0:BLOCK, :]]), band_halo)
    o0_s[0:BLOCK, :] = o
    l0_s[0:BLOCK, :] = lse

    def body0(n, carry):
        r0 = pl.multiple_of(n * BLOCK, BLOCK)
        rp = pl.multiple_of((n - 1) * BLOCK, BLOCK)
        o, lse = _band_block(q0_ref[pl.ds(r0, BLOCK), :], k0_ref[pl.ds(rp, 2 * BLOCK), :],
                             v0_ref[pl.ds(rp, 2 * BLOCK), :], band)
        o0_s[pl.ds(r0, BLOCK), :] = o
        l0_s[pl.ds(r0, BLOCK), :] = lse
        return carry
    lax.fori_loop(1, nblk, body0, 0)

    tin = TM_IN
    for r in range(4):
        for n in range(4):
            cur = slice(n * tin + r * BLOCK, n * tin + (r + 1) * BLOCK)
            if n == 0:
                kp, vp, msk = k1h_ref[r * BLOCK:(r + 1) * BLOCK, :], v1h_ref[r * BLOCK:(r + 1) * BLOCK, :], band_halo
            else:
                prev = slice((n - 1) * tin + r * BLOCK, (n - 1) * tin + (r + 1) * BLOCK)
                kp, vp, msk = k1_ref[prev, :], v1_ref[prev, :], band
            o, lse = _band_block(q1_ref[cur, :], cat([kp, k1_ref[cur, :]]), cat([vp, v1_ref[cur, :]]), msk)
            o1_s[pl.ds(n * tin + r, BLOCK, stride=4), :] = o
            l1_s[pl.ds(n * tin + r, BLOCK, stride=4), :] = lse

    piece = tin // 16

    def body2(r16, carry):
        off = pl.multiple_of(r16 * piece, piece)
        rows = lambda ref: [ref[pl.ds(c * tin + off, piece), :] for c in range(TILE_ATTN // tin)]
        qb = cat(rows(q2_ref))
        kc = cat(rows(k2h_ref) + rows(k2_ref))
        vc = cat(rows(v2h_ref) + rows(v2_ref))
        o, lse = _band_block(qb, kc, vc, band_halo)
        dst = (r16 % 4) * (TILE_ATTN // 4) + r16 // 4
        oa_s[pl.ds(dst, BLOCK, stride=4), :] = o
        la_s[pl.ds(dst, BLOCK, stride=4), :] = lse
        return carry
    lax.fori_loop(0, 16, body2, 0)
    quarter = TILE_ATTN // 4
    for r in range(4):
        o2_s[pl.ds(r, quarter, stride=4), :] = oa_s[r * quarter:(r + 1) * quarter, :]
        l2_s[pl.ds(r, quarter, stride=4), :] = la_s[r * quarter:(r + 1) * quarter, :]

    ch = 256
    for c in range(TILE_ATTN // ch):
        rs = slice(c * ch, (c + 1) * ch)
        la, lb, lc = l0_s[rs, :], l1_s[rs, :], l2_s[rs, :]
        mx = jnp.maximum(jnp.maximum(la, lb), lc)
        ea, eb, ec = jnp.exp(la - mx), jnp.exp(lb - mx), jnp.exp(lc - mx)
        inv = 1.0 / (ea + eb + ec)
        acc = (ea * inv) * o0_s[rs, :] + (eb * inv) * o1_s[rs, :] + (ec * inv) * o2_s[rs, :]
        o_ref[rs, :] = acc.astype(BF16)


def _attention(q, k, v):
    bsz, seq, _ = q.shape
    t = TILE_ATTN
    hpg = HEADS_PER_GROUP

    def cur(g):
        return pl.BlockSpec((None, t, HEAD_DIM), lambda b, i, j: (b, i, g * hpg + j))

    def halo(g, rows):
        per = t // rows
        return pl.BlockSpec((None, rows, HEAD_DIM),
                            lambda b, i, j: (b, jnp.maximum(i * per - 1, 0), g * hpg + j))

    halos = [halo(0, BLOCK), halo(1, TM_IN), halo(2, t)]
    scratch = [pltpu.VMEM((t, HEAD_DIM), F32)] * 8
    return pl.pallas_call(
        _attn_kernel,
        grid=(bsz, seq // t, hpg),
        in_specs=[cur(0), cur(1), cur(2)] * 3 + halos * 2,
        out_specs=pl.BlockSpec((None, t, HEAD_DIM), lambda b, i, j: (b, i, j)),
        out_shape=jax.ShapeDtypeStruct((bsz, seq, ATTN_OUT), BF16),
        scratch_shapes=scratch,
        compiler_params=_params(3),
        name="attention",
    )(q, q, q, k, k, k, v, v, v, k, k, k, v, v, v)


def _mix_kernel(attn_ref, u_ref, uh_ref, ga_ref, gb_ref, x_ref, mod_ref, wap_ref, wco_ref, wo_ref,
                wdw_ref, bdw_ref, gln_ref, bln_ref, o_ref, ext_s, y_s):
    tm = x_ref.shape[0]
    d = x_ref.shape[1]
    first = pl.program_id(1) == 0
    halo = uh_ref[...].astype(F32)
    ext_s[0:CONV_HALO, :] = jnp.where(first, 0.0, halo)
    ext_s[CONV_HALO:, :] = u_ref[...].astype(F32)

    rc = 64
    lead = CONV_HALO - (CONV_K - 1)

    for c in range(d // HEAD_DIM):
        cols = slice(c * HEAD_DIM, (c + 1) * HEAD_DIM)
        taps = [wdw_ref[kk:kk + 1, cols] for kk in range(CONV_K)]
        bias = bdw_ref[:, cols]
        for i in range(tm // rc):
            r0 = i * rc
            acc = bias + taps[0] * ext_s[r0 + lead:r0 + lead + rc, cols]
            for kk in range(1, CONV_K):
                acc = acc + taps[kk] * ext_s[r0 + lead + kk:r0 + lead + kk + rc, cols]
            y_s[r0:r0 + rc, cols] = acc

    y = y_s[...]
    mu = jnp.mean(y, axis=-1, keepdims=True)
    yc = y - mu
    var = jnp.mean(yc * yc, axis=-1, keepdims=True)
    yn = yc * lax.rsqrt(var + EPS) * gln_ref[...] + bln_ref[...]
    ub = (yn * _sigmoid(yn)).astype(BF16)
    y_b = jnp.dot(ub, wco_ref[...], preferred_element_type=F32)
    y_a = jnp.dot(attn_ref[...], wap_ref[...], preferred_element_type=F32)
    merged = ga_ref[...].astype(F32) * y_a + gb_ref[...].astype(F32) * y_b
    out = jnp.dot(merged.astype(BF16), wo_ref[...], preferred_element_type=F32)
    o_ref[...] = x_ref[...] + mod_ref[2:3, :] * out


def _mix(attn, u, ga, gb, x, mod, wap, wco, wo, wdw, bdw, gln, bln, layer):
    bsz, seq, d = x.shape
    tm = TM_MIX
    row = lambda b, i: (b, i, 0)
    par = lambda b, i: (layer, 0, 0)
    per = tm // CONV_HALO
    return pl.pallas_call(
        _mix_kernel,
        grid=(bsz, seq // tm),
        in_specs=[pl.BlockSpec((None, tm, ATTN_OUT), row),
                  pl.BlockSpec((None, tm, d), row),
                  pl.BlockSpec((None, CONV_HALO, d), lambda b, i: (b, jnp.maximum(i * per - 1, 0), 0)),
                  pl.BlockSpec((None, tm, d), row),
                  pl.BlockSpec((None, tm, d), row),
                  pl.BlockSpec((None, tm, d), row),
                  pl.BlockSpec((None, None, 6, d), lambda b, i: (layer, b, 0, 0)),
                  pl.BlockSpec((None, ATTN_OUT, d), par),
                  pl.BlockSpec((None, d, d), par),
                  pl.BlockSpec((None, d, d), par),
                  pl.BlockSpec((None, CONV_K, d), par),
                  pl.BlockSpec((None, 1, d), par),
                  pl.BlockSpec((None, 1, d), par),
                  pl.BlockSpec((None, 1, d), par)],
        out_specs=pl.BlockSpec((None, tm, d), row),
        out_shape=jax.ShapeDtypeStruct((bsz, seq, d), F32),
        scratch_shapes=[pltpu.VMEM((tm + CONV_HALO, d), F32), pltpu.VMEM((tm, d), F32)],
        compiler_params=_params(2),
        name="mix",
    )(attn, u, u, ga, gb, x, mod, wap, wco, wo, wdw, bdw, gln, bln)


def _ffn_kernel(x_ref, mod_ref, g2_ref, win_ref, wdw_ref, bdw_ref, wdn_ref, o_ref, ext_s, carry_s):
    tm = x_ref.shape[0]
    first = pl.program_id(1) == 0
    x = x_ref[...]
    h = _rms_mod(x, g2_ref[...], mod_ref[3:4, :], mod_ref[4:5, :]).astype(BF16)

    @pl.when(first)
    def _():
        carry_s[...] = jnp.zeros_like(carry_s)

    half = D_FF // 2
    acc = jnp.zeros((tm, D_MODEL), F32)
    for c in range(2):
        cols = slice(c * half, (c + 1) * half)
        g = jnp.dot(h, win_ref[:, c * half:(c + 1) * half], preferred_element_type=F32)
        up = jnp.dot(h, win_ref[:, D_FF + c * half:D_FF + (c + 1) * half], preferred_element_type=F32)
        ext_s[0:FFN_HALO, :] = carry_s[:, cols]
        ext_s[FFN_HALO:, :] = g
        carry_s[:, cols] = g[tm - FFN_HALO:, :]
        gc = bdw_ref[:, cols] + wdw_ref[2:3, cols] * g
        for kk in range(FFN_K - 1):
            gc = gc + wdw_ref[kk:kk + 1, cols] * ext_s[pl.ds(FFN_HALO - (FFN_K - 1) + kk, tm), :]
        act = (gc * _sigmoid(gc) * up).astype(BF16)
        acc = acc + jnp.dot(act, wdn_ref[c * half:(c + 1) * half, :], preferred_element_type=F32)
    o_ref[...] = x + mod_ref[5:6, :] * acc


def _ffn(x, mod, g2, win, wdw, bdw, wdn, layer):
    bsz, seq, d = x.shape
    tm = TM_FFN
    row = lambda b, i: (b, i, 0)
    par = lambda b, i: (layer, 0, 0)
    return pl.pallas_call(
        _ffn_kernel,
        grid=(bsz, seq // tm),
        in_specs=[pl.BlockSpec((None, tm, d), row),
                  pl.BlockSpec((None, None, 6, d), lambda b, i: (layer, b, 0, 0)),
                  pl.BlockSpec((None, 1, d), par),
                  pl.BlockSpec((None, d, 2 * D_FF), par, pipeline_mode=pl.Buffered(1)),
                  pl.BlockSpec((None, FFN_K, D_FF), par),
                  pl.BlockSpec((None, 1, D_FF), par),
                  pl.BlockSpec((None, D_FF, d), par, pipeline_mode=pl.Buffered(1))],
        out_specs=pl.BlockSpec((None, tm, d), row),
        out_shape=jax.ShapeDtypeStruct((bsz, seq, d), F32),
        scratch_shapes=[pltpu.VMEM((tm + FFN_HALO, D_FF // 2), F32), pltpu.VMEM((FFN_HALO, D_FF), F32)],
        compiler_params=_params(2),
        name="ffn",
    )(x, mod, g2, win, wdw, bdw, wdn)


def _rope_tables(positions):
    inv_freq = ROPE_THETA ** (-jnp.arange(0, ROT_DIM, 2, dtype=F32) / ROT_DIM)
    ang = positions.astype(F32)[..., None] * inv_freq
    cos, sin = jnp.cos(ang), jnp.sin(ang)
    pad = HEAD_DIM - ROT_DIM
    cos_tab = jnp.concatenate([cos, cos, jnp.ones(cos.shape[:-1] + (pad,), F32)], axis=-1)
    sin_tab = jnp.concatenate([-sin, sin, jnp.zeros(sin.shape[:-1] + (pad,), F32)], axis=-1)
    return cos_tab, sin_tab


def _rope_partner(g):
    return jnp.concatenate([g[:, ROT_HALF:ROT_DIM], g[:, :ROT_HALF], g[:, ROT_DIM:]], axis=-1)


def kernel(x, c, positions, w_ada, b_ada, g_norm1, w_in, g_q, g_k, w_attn_proj, w_conv_dw, b_conv_dw,
           g_conv_ln, b_conv_ln, w_conv_out, w_o, g_norm2, w_ffn_in, w_ffn_dw, b_ffn_dw, w_ffn_down):
    depth = w_ada.shape[0]
    bsz, seq, d = x.shape
    assert (d, seq % TILE_ATTN) == (D_MODEL, 0)

    mod = _ada(c, w_ada, b_ada).reshape(depth, bsz, 6, d)
    cos_tab, sin_tab = _rope_tables(positions)
    vec = lambda a: a.reshape(depth, 1, a.shape[-1])
    w_in_b, wap_b, wco_b, wo_b = (w.astype(BF16) for w in (w_in, w_attn_proj, w_conv_out, w_o))
    win_b, wdn_b = w_ffn_in.astype(BF16), w_ffn_down.astype(BF16)
    g1, g2, gq, gk = vec(g_norm1), vec(g_norm2), vec(g_q), vec(g_k)
    pgq, pgk = vec(_rope_partner(g_q)), vec(_rope_partner(g_k))
    bdw, gln, bln, bfdw = vec(b_conv_dw), vec(g_conv_ln), vec(b_conv_ln), vec(b_ffn_dw)

    for layer in range(depth):
        q, k, v, u, ga, gb = _in_proj(x, mod, g1, w_in_b, gq, gk, pgq, pgk, cos_tab, sin_tab, layer)
        attn = _attention(q, k, v)
        x = _mix(attn, u, ga, gb, x, mod, wap_b, wco_b, wo_b, w_conv_dw, bdw, gln, bln, layer)
        x = _ffn(x, mod, g2, win_b, w_ffn_dw, bfdw, wdn_b, layer)
    return x
```

```python
import functools

import jax
import jax.numpy as jnp
from jax import lax
from jax.experimental import pallas as pl
from jax.experimental.pallas import tpu as pltpu

F32 = jnp.float32
BF16 = jnp.bfloat16

D_MODEL = 1024
HEAD_DIM = 128
DILATIONS = (1, 4, 16)
HEADS_PER_GROUP = 4
N_HEADS = HEADS_PER_GROUP * len(DILATIONS)
ATTN_WIDTH = N_HEADS * HEAD_DIM
ATTN_OUT = HEADS_PER_GROUP * HEAD_DIM
BLOCK = 128
ROT_DIM = HEAD_DIM // 4
ROT_HALF = ROT_DIM // 2
ROPE_THETA = 500000.0
CONV_K = 31
D_FF = 2816
FFN_K = 3
IN_WIDTH = 3 * ATTN_WIDTH + 4 * D_MODEL
EPS = 1e-6
NEG = -1e30
Q_SCALE = HEAD_DIM ** -0.5 * 1.4426950408889634

TM_IN = 512
TILE_ATTN = 2048
TM_MIX = 512
TM_FFN = 512
CONV_HALO = 32
FFN_HALO = 8
FFN_CHUNK = 768
V7X_VMEM_LIMIT = 56 * 1024 * 1024


def _params(n_axes):
    return pltpu.CompilerParams(dimension_semantics=("arbitrary",) * n_axes,
                                vmem_limit_bytes=V7X_VMEM_LIMIT)


def _sigmoid(x):
    return 1.0 / (1.0 + jnp.exp(-x))


def _ada_kernel(c_ref, w_ref, b_ref, o_ref):
    c = c_ref[...]
    c_act = c * _sigmoid(c)
    o_ref[...] = jnp.dot(c_act, w_ref[...], preferred_element_type=F32) + b_ref[...]


def _ada(c, w_ada, b_ada):
    depth, d, n = w_ada.shape
    bsz = c.shape[0]
    nt = 1024
    return pl.pallas_call(
        _ada_kernel,
        grid=(depth, n // nt),
        in_specs=[pl.BlockSpec((bsz, d), lambda l, j: (0, 0)),
                  pl.BlockSpec((None, d, nt), lambda l, j: (l, 0, j)),
                  pl.BlockSpec((None, 1, nt), lambda l, j: (l, 0, j))],
        out_specs=pl.BlockSpec((None, bsz, nt), lambda l, j: (l, 0, j)),
        out_shape=jax.ShapeDtypeStruct((depth, bsz, n), F32),
        compiler_params=_params(2),
        name="ada",
    )(c, w_ada, b_ada.reshape(depth, 1, n))


def _rms_mod(x, g, shift, scale):
    ms = jnp.mean(x * x, axis=-1, keepdims=True)
    h = x * lax.rsqrt(ms + EPS) * g
    return h * (1.0 + scale) + shift


def _in_proj_kernel(x_ref, mod_ref, g1_ref, w_ref, gq_ref, gk_ref, pgq_ref, pgk_ref, cos_ref, sin_ref,
                    q_ref, k_ref, v_ref, u_ref, ga_ref, gb_ref, zs_ref, zs2_ref):
    tm = x_ref.shape[0]
    h = _rms_mod(x_ref[...], g1_ref[...], mod_ref[0:1, :], mod_ref[1:2, :]).astype(BF16)

    cos = cos_ref[...]
    sin = sin_ref[...]
    lane = lax.broadcasted_iota(jnp.int32, (tm, HEAD_DIM), 1)
    low = lane < ROT_HALF

    def store_head(o_ref, col, val, dil, slot):
        cols = slice(col, col + HEAD_DIM)
        if dil == 1:
            o_ref[:, cols] = val.astype(BF16)
            return
        zs = zs_ref.at[slot]
        zs[...] = val
        per = tm // 4
        if dil == 4:
            for r in range(4):
                o_ref[r * per:(r + 1) * per, cols] = zs[pl.ds(r, per, stride=4), :].astype(BF16)
            return
        zs2 = zs2_ref.at[slot]
        for r in range(4):
            zs2[r * per:(r + 1) * per, :] = zs[pl.ds(r, per, stride=4), :]
        per2 = per // 4
        for r in range(4):
            for rp in range(4):
                r16 = r + 4 * rp
                o_ref[r16 * per2:(r16 + 1) * per2, cols] = (
                    zs2[pl.ds(r * per + rp, per2, stride=4), :].astype(BF16))

    def norm_rope(t, a_tab, b_tab, post):
        ms = jnp.mean(t * t, axis=-1, keepdims=True)
        rinv = lax.rsqrt(ms + EPS) * post
        partner = jnp.where(low, pltpu.roll(t, HEAD_DIM - ROT_HALF, 1), pltpu.roll(t, ROT_HALF, 1))
        return (t * a_tab + partner * b_tab) * rinv

    slot = 0
    for which, (o_ref, g_ref, pg_ref, post) in enumerate(
            ((q_ref, gq_ref, pgq_ref, Q_SCALE), (k_ref, gk_ref, pgk_ref, 1.0), (v_ref, None, None, None))):
        if g_ref is not None:
            a_tab = cos * g_ref[...]
            b_tab = sin * pg_ref[...]
        for gi, dil in enumerate(DILATIONS):
            base = which * ATTN_WIDTH + gi * ATTN_OUT
            z = jnp.dot(h, w_ref[:, base:base + ATTN_OUT], preferred_element_type=F32)
            for hh in range(HEADS_PER_GROUP):
                t = z[:, hh * HEAD_DIM:(hh + 1) * HEAD_DIM]
                if g_ref is not None:
                    t = norm_rope(t, a_tab, b_tab, post)
                store_head(o_ref, gi * ATTN_OUT + hh * HEAD_DIM, t, dil, slot % zs_ref.shape[0])
                slot += 1

    cw = 512
    base_val = 3 * ATTN_WIDTH
    base_gate = base_val + D_MODEL
    for cc in range(D_MODEL // cw):
        val = jnp.dot(h, w_ref[:, base_val + cc * cw:base_val + (cc + 1) * cw], preferred_element_type=F32)
        gate = jnp.dot(h, w_ref[:, base_gate + cc * cw:base_gate + (cc + 1) * cw], preferred_element_type=F32)
        u_ref[:, cc * cw:(cc + 1) * cw] = (val * _sigmoid(gate)).astype(BF16)
    base_a = base_gate + D_MODEL
    base_b = base_a + D_MODEL
    for cc in range(D_MODEL // cw):
        za = jnp.dot(h, w_ref[:, base_a + cc * cw:base_a + (cc + 1) * cw], preferred_element_type=F32)
        ga_ref[:, cc * cw:(cc + 1) * cw] = _sigmoid(za).astype(BF16)
        zb = jnp.dot(h, w_ref[:, base_b + cc * cw:base_b + (cc + 1) * cw], preferred_element_type=F32)
        gb_ref[:, cc * cw:(cc + 1) * cw] = _sigmoid(zb).astype(BF16)


def _in_proj(x, mod, g1, w_in, gq, gk, pgq, pgk, cos_tab, sin_tab, layer):
    bsz, seq, d = x.shape
    tm = TM_IN
    n_slots = 4
    row = lambda b, i: (b, i, 0)
    par = lambda b, i: (layer, 0, 0)
    out_attn = jax.ShapeDtypeStruct((bsz, seq, ATTN_WIDTH), BF16)
    out_d = jax.ShapeDtypeStruct((bsz, seq, d), BF16)
    return pl.pallas_call(
        _in_proj_kernel,
        grid=(bsz, seq // tm),
        in_specs=[pl.BlockSpec((None, tm, d), row),
                  pl.BlockSpec((None, None, 6, d), lambda b, i: (layer, b, 0, 0)),
                  pl.BlockSpec((None, 1, d), par),
                  pl.BlockSpec((None, d, IN_WIDTH), par, pipeline_mode=pl.Buffered(1)),
                  pl.BlockSpec((None, 1, HEAD_DIM), par),
                  pl.BlockSpec((None, 1, HEAD_DIM), par),
                  pl.BlockSpec((None, 1, HEAD_DIM), par),
                  pl.BlockSpec((None, 1, HEAD_DIM), par),
                  pl.BlockSpec((None, tm, HEAD_DIM), row),
                  pl.BlockSpec((None, tm, HEAD_DIM), row)],
        out_specs=[pl.BlockSpec((None, tm, ATTN_WIDTH), row)] * 3 + [pl.BlockSpec((None, tm, d), row)] * 3,
        out_shape=[out_attn] * 3 + [out_d] * 3,
        scratch_shapes=[pltpu.VMEM((n_slots, tm, HEAD_DIM), F32), pltpu.VMEM((n_slots, tm, HEAD_DIM), F32)],
        compiler_params=_params(2),
        name="in_proj",
    )(x, mod, g1, w_in, gq, gk, pgq, pgk, cos_tab, sin_tab)


def _band_block(q, kc, vc, bias):
    s = lax.dot_general(q, kc, (((1,), (1,)), ((), ())), preferred_element_type=F32) + bias
    m = jnp.max(s, axis=-1, keepdims=True)
    p = jnp.exp2(s - m).astype(BF16)
    ones = jnp.ones((2 * BLOCK, HEAD_DIM), BF16)
    ol = jnp.dot(p, jnp.concatenate([vc, ones], axis=1), preferred_element_type=F32)
    return ol[:, :HEAD_DIM], jnp.broadcast_to(m, (BLOCK, HEAD_DIM)), ol[:, HEAD_DIM:]


def _attn_kernel(q0_ref, q1_ref, q2_ref, k0_ref, k1_ref, k2_ref, v0_ref, v1_ref, v2_ref,
                 k0h_ref, k1h_ref, k2h_ref, v0h_ref, v1h_ref, v2h_ref, o_ref,
                 a0_s, m0_s, l0_s, a1_s, m1_s, l1_s, a2_s, m2_s, l2_s, as_s, ms_s, ls_s, bias_s):
    first = pl.program_id(1) == 0
    qi = lax.broadcasted_iota(jnp.int32, (BLOCK, 2 * BLOCK), 0)
    kj = lax.broadcasted_iota(jnp.int32, (BLOCK, 2 * BLOCK), 1)
    band = (kj >= qi) & (kj <= qi + BLOCK)
    bias_s[0] = jnp.where(band, 0.0, NEG)
    bias_s[1] = jnp.where(band & (kj >= jnp.where(first, BLOCK, 0)), 0.0, NEG)
    inner, edge = 0, 1

    nblk = TILE_ATTN // BLOCK
    tin = TM_IN
    cat = functools.partial(jnp.concatenate, axis=0)

    def put(refs, idx, vals):
        for ref, val in zip(refs, vals):
            ref[idx, :] = val

    def rows(ref, n):
        return ref[n * BLOCK:(n + 1) * BLOCK, :]

    g0 = (a0_s, m0_s, l0_s)
    for n in range(nblk):
        if n == 0:
            kc, vc, which = cat([k0h_ref[...], rows(k0_ref, 0)]), cat([v0h_ref[...], rows(v0_ref, 0)]), edge
        else:
            win = slice((n - 1) * BLOCK, (n + 1) * BLOCK)
            kc, vc, which = k0_ref[win, :], v0_ref[win, :], inner
        put(g0, slice(n * BLOCK, (n + 1) * BLOCK), _band_block(rows(q0_ref, n), kc, vc, bias_s[which]))

    g1 = (a1_s, m1_s, l1_s)
    per_tile = tin // BLOCK
    for r in range(4):
        for n in range(TILE_ATTN // tin):
            cur = n * per_tile + r
            if n == 0:
                kp, vp, which = rows(k1h_ref, r), rows(v1h_ref, r), edge
            else:
                kp, vp, which = rows(k1_ref, cur - per_tile), rows(v1_ref, cur - per_tile), inner
            res = _band_block(rows(q1_ref, cur), cat([kp, rows(k1_ref, cur)]), cat([vp, rows(v1_ref, cur)]),
                              bias_s[which])
            put(g1, pl.ds(n * tin + r, BLOCK, stride=4), res)

    piece = tin // 16
    gs = (as_s, ms_s, ls_s)
    for r16 in range(16):
        pieces = lambda ref: [ref[c * tin + r16 * piece:c * tin + (r16 + 1) * piece, :]
                              for c in range(TILE_ATTN // tin)]
        res = _band_block(cat(pieces(q2_ref)), cat(pieces(k2h_ref) + pieces(k2_ref)),
                          cat(pieces(v2h_ref) + pieces(v2_ref)), bias_s[edge])
        put(gs, pl.ds((r16 % 4) * (TILE_ATTN // 4) + r16 // 4, BLOCK, stride=4), res)
    quarter = TILE_ATTN // 4
    for r in range(4):
        for dst, src in zip((a2_s, m2_s, l2_s), gs):
            dst[pl.ds(r, quarter, stride=4), :] = src[r * quarter:(r + 1) * quarter, :]

    ch = 256
    for c in range(TILE_ATTN // ch):
        rs = slice(c * ch, (c + 1) * ch)
        ma, mb, mc = m0_s[rs, :], m1_s[rs, :], m2_s[rs, :]
        mx = jnp.maximum(jnp.maximum(ma, mb), mc)
        ea, eb, ec = jnp.exp2(ma - mx), jnp.exp2(mb - mx), jnp.exp2(mc - mx)
        den = ea * l0_s[rs, :] + eb * l1_s[rs, :] + ec * l2_s[rs, :]
        num = ea * a0_s[rs, :] + eb * a1_s[rs, :] + ec * a2_s[rs, :]
        o_ref[rs, :] = (num * (1.0 / den)).astype(BF16)


def _attention(q, k, v):
    bsz, seq, _ = q.shape
    t = TILE_ATTN
    hpg = HEADS_PER_GROUP

    def cur(g):
        return pl.BlockSpec((None, t, HEAD_DIM), lambda b, i, j: (b, i, g * hpg + j))

    def halo(g, rows):
        per = t // rows
        return pl.BlockSpec((None, rows, HEAD_DIM),
                            lambda b, i, j: (b, jnp.maximum(i * per - 1, 0), g * hpg + j))

    halos = [halo(0, BLOCK), halo(1, TM_IN), halo(2, t)]
    scratch = [pltpu.VMEM((t, HEAD_DIM), F32)] * 12 + [pltpu.VMEM((2, BLOCK, 2 * BLOCK), F32)]
    return pl.pallas_call(
        _attn_kernel,
        grid=(bsz, seq // t, hpg),
        in_specs=[cur(0), cur(1), cur(2)] * 3 + halos * 2,
        out_specs=pl.BlockSpec((None, t, HEAD_DIM), lambda b, i, j: (b, i, j)),
        out_shape=jax.ShapeDtypeStruct((bsz, seq, ATTN_OUT), BF16),
        scratch_shapes=scratch,
        compiler_params=_params(3),
        name="attention",
    )(q, q, q, k, k, k, v, v, v, k, k, k, v, v, v)


def _mix_kernel(attn_ref, u_ref, uh_ref, ga_ref, gb_ref, x_ref, mod_ref, wap_ref, wco_ref, wo_ref,
                wdw_ref, bdw_ref, gln_ref, bln_ref, o_ref, ext_s, y_s, sh_s):
    tm = x_ref.shape[0]
    d = x_ref.shape[1]
    first = pl.program_id(1) == 0
    halo = uh_ref[...].astype(F32)
    ext_s[0:CONV_HALO, :] = jnp.where(first, 0.0, halo)
    ext_s[CONV_HALO:, :] = u_ref[...].astype(F32)

    rc = 64
    lead = CONV_HALO - (CONV_K - 1)
    rows_sh = sh_s.shape[2]
    for c in range(d // HEAD_DIM):
        cols = slice(c * HEAD_DIM, (c + 1) * HEAD_DIM)
        sh = sh_s.at[c % sh_s.shape[0]]
        for s in range(1, 8):
            sh[s - 1, :, :] = ext_s[s:s + rows_sh, cols]
        taps = [jnp.broadcast_to(wdw_ref[kk:kk + 1, cols], (rc, HEAD_DIM)) for kk in range(CONV_K)]
        bias = jnp.broadcast_to(bdw_ref[:, cols], (rc, HEAD_DIM))
        for i in range(tm // rc):
            acc = bias
            for kk in range(CONV_K):
                r0 = i * rc + (lead + kk) // 8 * 8
                s = (lead + kk) % 8
                win = ext_s[r0:r0 + rc, cols] if s == 0 else sh[s - 1, r0:r0 + rc, :]
                acc = acc + taps[kk] * win
            y_s[i * rc:(i + 1) * rc, cols] = acc

    y = y_s[...]
    mu = jnp.mean(y, axis=-1, keepdims=True)
    yc = y - mu
    var = jnp.mean(yc * yc, axis=-1, keepdims=True)
    yn = yc * lax.rsqrt(var + EPS) * gln_ref[...] + bln_ref[...]
    ub = (yn * _sigmoid(yn)).astype(BF16)
    y_b = jnp.dot(ub, wco_ref[...], preferred_element_type=F32)
    y_a = jnp.dot(attn_ref[...], wap_ref[...], preferred_element_type=F32)
    merged = ga_ref[...].astype(F32) * y_a + gb_ref[...].astype(F32) * y_b
    out = jnp.dot(merged.astype(BF16), wo_ref[...], preferred_element_type=F32)
    o_ref[...] = x_ref[...] + mod_ref[2:3, :] * out


def _mix(attn, u, ga, gb, x, mod, wap, wco, wo, wdw, bdw, gln, bln, layer):
    bsz, seq, d = x.shape
    tm = TM_MIX
    row = lambda b, i: (b, i, 0)
    par = lambda b, i: (layer, 0, 0)
    per = tm // CONV_HALO
    return pl.pallas_call(
        _mix_kernel,
        grid=(bsz, seq // tm),
        in_specs=[pl.BlockSpec((None, tm, ATTN_OUT), row),
                  pl.BlockSpec((None, tm, d), row),
                  pl.BlockSpec((None, CONV_HALO, d), lambda b, i: (b, jnp.maximum(i * per - 1, 0), 0)),
                  pl.BlockSpec((None, tm, d), row),
                  pl.BlockSpec((None, tm, d), row),
                  pl.BlockSpec((None, tm, d), row),
                  pl.BlockSpec((None, None, 6, d), lambda b, i: (layer, b, 0, 0)),
                  pl.BlockSpec((None, ATTN_OUT, d), par),
                  pl.BlockSpec((None, d, d), par),
                  pl.BlockSpec((None, d, d), par),
                  pl.BlockSpec((None, CONV_K, d), par),
                  pl.BlockSpec((None, 1, d), par),
                  pl.BlockSpec((None, 1, d), par),
                  pl.BlockSpec((None, 1, d), par)],
        out_specs=pl.BlockSpec((None, tm, d), row),
        out_shape=jax.ShapeDtypeStruct((bsz, seq, d), F32),
        scratch_shapes=[pltpu.VMEM((tm + CONV_HALO, d), F32), pltpu.VMEM((tm, d), F32),
                        pltpu.VMEM((2, 7, tm + CONV_HALO - 8, HEAD_DIM), F32)],
        compiler_params=_params(2),
        name="mix",
    )(attn, u, u, ga, gb, x, mod, wap, wco, wo, wdw, bdw, gln, bln)


def _ffn_kernel(x_ref, mod_ref, g2_ref, win_ref, wdw_ref, bdw_ref, wdn_ref, o_ref, ext_s, carry_s, sh_s, act_s):
    tm = x_ref.shape[0]
    first = pl.program_id(1) == 0
    x = x_ref[...]
    h = _rms_mod(x, g2_ref[...], mod_ref[3:4, :], mod_ref[4:5, :]).astype(BF16)

    @pl.when(first)
    def _():
        carry_s[...] = jnp.zeros_like(carry_s)

    cw = FFN_CHUNK
    for c0 in range(0, D_FF, cw):
        w = min(cw, D_FF - c0)
        cols = slice(c0, c0 + w)
        g = jnp.dot(h, win_ref[:, c0:c0 + w], preferred_element_type=F32)
        up = jnp.dot(h, win_ref[:, D_FF + c0:D_FF + c0 + w], preferred_element_type=F32)
        ext_s[0:FFN_HALO, 0:w] = carry_s[:, cols]
        ext_s[FFN_HALO:, 0:w] = g
        carry_s[:, cols] = g[tm - FFN_HALO:, :]
        gc = bdw_ref[:, cols] + wdw_ref[2:3, cols] * g
        for kk in range(FFN_K - 1):
            start = FFN_HALO - (FFN_K - 1) + kk
            sh_s[kk, :, 0:w] = ext_s[start:start + tm, 0:w]
            gc = gc + wdw_ref[kk:kk + 1, cols] * sh_s[kk, :, 0:w]
        act_s[:, cols] = (gc * _sigmoid(gc) * up).astype(BF16)
    out = jnp.dot(act_s[...], wdn_ref[...], preferred_element_type=F32)
    o_ref[...] = x + mod_ref[5:6, :] * out


def _ffn(x, mod, g2, win, wdw, bdw, wdn, layer):
    bsz, seq, d = x.shape
    tm = TM_FFN
    row = lambda b, i: (b, i, 0)
    par = lambda b, i: (layer, 0, 0)
    return pl.pallas_call(
        _ffn_kernel,
        grid=(bsz, seq // tm),
        in_specs=[pl.BlockSpec((None, tm, d), row),
                  pl.BlockSpec((None, None, 6, d), lambda b, i: (layer, b, 0, 0)),
                  pl.BlockSpec((None, 1, d), par),
                  pl.BlockSpec((None, d, 2 * D_FF), par, pipeline_mode=pl.Buffered(1)),
                  pl.BlockSpec((None, FFN_K, D_FF), par),
                  pl.BlockSpec((None, 1, D_FF), par),
                  pl.BlockSpec((None, D_FF, d), par, pipeline_mode=pl.Buffered(1))],
        out_specs=pl.BlockSpec((None, tm, d), row),
        out_shape=jax.ShapeDtypeStruct((bsz, seq, d), F32),
        scratch_shapes=[pltpu.VMEM((tm + FFN_HALO, FFN_CHUNK), F32), pltpu.VMEM((FFN_HALO, D_FF), F32),
                        pltpu.VMEM((FFN_K - 1, tm, FFN_CHUNK), F32), pltpu.VMEM((tm, D_FF), BF16)],
        compiler_params=_params(2),
        name="ffn",
    )(x, mod, g2, win, wdw, bdw, wdn)


def _rope_tables(positions):
    inv_freq = ROPE_THETA ** (-jnp.arange(0, ROT_DIM, 2, dtype=F32) / ROT_DIM)
    ang = positions.astype(F32)[..., None] * inv_freq
    cos, sin = jnp.cos(ang), jnp.sin(ang)
    pad = HEAD_DIM - ROT_DIM
    cos_tab = jnp.concatenate([cos, cos, jnp.ones(cos.shape[:-1] + (pad,), F32)], axis=-1)
    sin_tab = jnp.concatenate([-sin, sin, jnp.zeros(sin.shape[:-1] + (pad,), F32)], axis=-1)
    return cos_tab, sin_tab


def _rope_partner(g):
    return jnp.concatenate([g[:, ROT_HALF:ROT_DIM], g[:, :ROT_HALF], g[:, ROT_DIM:]], axis=-1)


def kernel(x, c, positions, w_ada, b_ada, g_norm1, w_in, g_q, g_k, w_attn_proj, w_conv_dw, b_conv_dw,
           g_conv_ln, b_conv_ln, w_conv_out, w_o, g_norm2, w_ffn_in, w_ffn_dw, b_ffn_dw, w_ffn_down):
    depth = w_ada.shape[0]
    bsz, seq, d = x.shape
    assert (d, seq % TILE_ATTN) == (D_MODEL, 0)

    mod = _ada(c, w_ada, b_ada).reshape(depth, bsz, 6, d)
    cos_tab, sin_tab = _rope_tables(positions)
    vec = lambda a: a.reshape(depth, 1, a.shape[-1])
    w_in_b, wap_b, wco_b, wo_b = (w.astype(BF16) for w in (w_in, w_attn_proj, w_conv_out, w_o))
    win_b, wdn_b = w_ffn_in.astype(BF16), w_ffn_down.astype(BF16)
    g1, g2, gq, gk = vec(g_norm1), vec(g_norm2), vec(g_q), vec(g_k)
    pgq, pgk = vec(_rope_partner(g_q)), vec(_rope_partner(g_k))
    bdw, gln, bln, bfdw = vec(b_conv_dw), vec(g_conv_ln), vec(b_conv_ln), vec(b_ffn_dw)

    for layer in range(depth):
        q, k, v, u, ga, gb = _in_proj(x, mod, g1, w_in_b, gq, gk, pgq, pgk, cos_tab, sin_tab, layer)
        attn = _attention(q, k, v)
        x = _mix(attn, u, ga, gb, x, mod, wap_b, wco_b, wo_b, w_conv_dw, bdw, gln, bln, layer)
        x = _ffn(x, mod, g2, win_b, w_ffn_dw, bfdw, wdn_b, layer)
    return x
```

```python
import functools

import jax
import jax.numpy as jnp
from jax import lax
from jax.experimental import pallas as pl
from jax.experimental.pallas import tpu as pltpu

F32 = jnp.float32
BF16 = jnp.bfloat16

D_MODEL = 1024
HEAD_DIM = 128
DILATIONS = (1, 4, 16)
HEADS_PER_GROUP = 4
N_HEADS = HEADS_PER_GROUP * len(DILATIONS)
ATTN_WIDTH = N_HEADS * HEAD_DIM
ATTN_OUT = HEADS_PER_GROUP * HEAD_DIM
BLOCK = 128
ROT_DIM = HEAD_DIM // 4
ROT_HALF = ROT_DIM // 2
ROT_SHIFT = HEAD_DIM // 2
HEAD_PERM = (tuple(range(ROT_HALF)) + tuple(range(ROT_DIM, ROT_SHIFT + ROT_HALF))
             + tuple(range(ROT_HALF, ROT_DIM)) + tuple(range(ROT_SHIFT + ROT_HALF, HEAD_DIM)))
ROPE_THETA = 500000.0
CONV_K = 31
D_FF = 2816
FFN_K = 3
IN_WIDTH = 3 * ATTN_WIDTH + 4 * D_MODEL
EPS = 1e-6
NEG = -1e30
Q_SCALE = HEAD_DIM ** -0.5 * 1.4426950408889634

TM_IN = 512
TILE_ATTN = 2048
TM_FFN = 512
CONV_HALO = 32
FFN_HALO = 8
FFN_CHUNK = 768
V7X_VMEM_LIMIT = 56 * 1024 * 1024


def _params(n_axes):
    return pltpu.CompilerParams(dimension_semantics=("arbitrary",) * n_axes,
                                vmem_limit_bytes=V7X_VMEM_LIMIT)


def _sigmoid(x):
    return 1.0 / (1.0 + jnp.exp(-x))


def _ada_kernel(c_ref, w_ref, b_ref, o_ref):
    c = c_ref[...]
    c_act = c * _sigmoid(c)
    o_ref[...] = jnp.dot(c_act, w_ref[...], preferred_element_type=F32) + b_ref[...]


def _ada(c, w_ada, b_ada):
    depth, d, n = w_ada.shape
    bsz = c.shape[0]
    nt = 1024
    return pl.pallas_call(
        _ada_kernel,
        grid=(depth, n // nt),
        in_specs=[pl.BlockSpec((bsz, d), lambda l, j: (0, 0)),
                  pl.BlockSpec((None, d, nt), lambda l, j: (l, 0, j)),
                  pl.BlockSpec((None, 1, nt), lambda l, j: (l, 0, j))],
        out_specs=pl.BlockSpec((None, bsz, nt), lambda l, j: (l, 0, j)),
        out_shape=jax.ShapeDtypeStruct((depth, bsz, n), F32),
        compiler_params=_params(2),
        name="ada",
    )(c, w_ada, b_ada.reshape(depth, 1, n))


def _rms_mod(x, g, shift, scale):
    ms = jnp.mean(x * x, axis=-1, keepdims=True)
    h = x * lax.rsqrt(ms + EPS) * g
    return h * (1.0 + scale) + shift


def _dwconv31_chunk(c, ext_s, sh_s, y_s, wdw_ref, bdw_ref, tm):
    rc = 64
    lead = CONV_HALO - (CONV_K - 1)
    rows_sh = sh_s.shape[2]
    cols = slice(c * HEAD_DIM, (c + 1) * HEAD_DIM)
    sh = sh_s.at[c % sh_s.shape[0]]
    for s in range(1, 8):
        sh[s - 1, :, :] = ext_s[s:s + rows_sh, cols]
    taps = [jnp.broadcast_to(wdw_ref[kk:kk + 1, cols], (rc, HEAD_DIM)) for kk in range(CONV_K)]
    bias = jnp.broadcast_to(bdw_ref[:, cols], (rc, HEAD_DIM))
    for i in range(tm // rc):
        acc = bias
        for kk in range(CONV_K):
            r0 = i * rc + (lead + kk) // 8 * 8
            s = (lead + kk) % 8
            win = ext_s[r0:r0 + rc, cols] if s == 0 else sh[s - 1, r0:r0 + rc, :]
            acc = acc + taps[kk] * win
        y_s[i * rc:(i + 1) * rc, cols] = acc


def _in_proj_kernel(x_ref, mod_ref, g1_ref, w_ref, gq_ref, gk_ref, pgq_ref, pgk_ref, cos_ref, sin_ref,
                    wdw_ref, bdw_ref, gln_ref, bln_ref, wco_ref,
                    q_ref, k_ref, v_ref, ga_ref, gyb_ref,
                    zs_ref, zs2_ref, ext_s, y_s, sh_s, carry_s):
    tm = x_ref.shape[0]
    h = _rms_mod(x_ref[...], g1_ref[...], mod_ref[0:1, :], mod_ref[1:2, :]).astype(BF16)

    @pl.when(pl.program_id(1) == 0)
    def _():
        carry_s[...] = jnp.zeros_like(carry_s)

    cw = 512
    base_val = 3 * ATTN_WIDTH
    base_gate = base_val + D_MODEL
    ext_s[0:CONV_HALO, :] = carry_s[...]
    for cc in range(D_MODEL // cw):
        val = jnp.dot(h, w_ref[:, base_val + cc * cw:base_val + (cc + 1) * cw], preferred_element_type=F32)
        gate = jnp.dot(h, w_ref[:, base_gate + cc * cw:base_gate + (cc + 1) * cw], preferred_element_type=F32)
        ext_s[CONV_HALO:, cc * cw:(cc + 1) * cw] = val * _sigmoid(gate)
    carry_s[...] = ext_s[tm:tm + CONV_HALO, :]
    conv_chunks = list(range(D_MODEL // HEAD_DIM))

    cos = cos_ref[...]
    sin = sin_ref[...]

    def store_head(o_ref, col, val, dil, slot):
        cols = slice(col, col + HEAD_DIM)
        if dil == 1:
            o_ref[:, cols] = val.astype(BF16)
            return
        zs = zs_ref.at[slot]
        zs[...] = val
        per = tm // 4
        if dil == 4:
            for r in range(4):
                o_ref[r * per:(r + 1) * per, cols] = zs[pl.ds(r, per, stride=4), :].astype(BF16)
            return
        zs2 = zs2_ref.at[slot]
        for r in range(4):
            zs2[r * per:(r + 1) * per, :] = zs[pl.ds(r, per, stride=4), :]
        per2 = per // 4
        for r in range(4):
            for rp in range(4):
                r16 = r + 4 * rp
                o_ref[r16 * per2:(r16 + 1) * per2, cols] = (
                    zs2[pl.ds(r * per + rp, per2, stride=4), :].astype(BF16))

    def norm_rope(t, a_tab, b_tab, post):
        ms = jnp.mean(t * t, axis=-1, keepdims=True)
        rinv = lax.rsqrt(ms + EPS) * post
        return (t * a_tab + pltpu.roll(t, ROT_SHIFT, 1) * b_tab) * rinv

    slot = 0
    for which, (o_ref, g_ref, pg_ref, post) in enumerate(
            ((q_ref, gq_ref, pgq_ref, Q_SCALE), (k_ref, gk_ref, pgk_ref, 1.0), (v_ref, None, None, None))):
        if g_ref is not None:
            a_tab = cos * g_ref[...]
            b_tab = sin * pg_ref[...]
        for gi, dil in enumerate(DILATIONS):
            base = which * ATTN_WIDTH + gi * ATTN_OUT
            z = jnp.dot(h, w_ref[:, base:base + ATTN_OUT], preferred_element_type=F32)
            for hh in range(HEADS_PER_GROUP):
                t = z[:, hh * HEAD_DIM:(hh + 1) * HEAD_DIM]
                if g_ref is not None:
                    t = norm_rope(t, a_tab, b_tab, post)
                store_head(o_ref, gi * ATTN_OUT + hh * HEAD_DIM, t, dil, slot % zs_ref.shape[0])
                slot += 1
            if conv_chunks:
                _dwconv31_chunk(conv_chunks.pop(0), ext_s, sh_s, y_s, wdw_ref, bdw_ref, tm)

    y = y_s[...]
    mu = jnp.mean(y, axis=-1, keepdims=True)
    yc = y - mu
    var = jnp.mean(yc * yc, axis=-1, keepdims=True)
    yn = yc * lax.rsqrt(var + EPS) * gln_ref[...] + bln_ref[...]
    ub = (yn * _sigmoid(yn)).astype(BF16)
    base_a = base_gate + D_MODEL
    base_b = base_a + D_MODEL
    for cc in range(D_MODEL // cw):
        cols = slice(cc * cw, (cc + 1) * cw)
        y_b = jnp.dot(ub, wco_ref[:, cols], preferred_element_type=F32)
        zb = jnp.dot(h, w_ref[:, base_b + cc * cw:base_b + (cc + 1) * cw], preferred_element_type=F32)
        gyb_ref[:, cols] = (_sigmoid(zb) * y_b).astype(BF16)
        za = jnp.dot(h, w_ref[:, base_a + cc * cw:base_a + (cc + 1) * cw], preferred_element_type=F32)
        ga_ref[:, cols] = _sigmoid(za).astype(BF16)


def _in_proj(x, mod, g1, w_in, gq, gk, pgq, pgk, cos_tab, sin_tab, wdw, bdw, gln, bln, wco, layer):
    bsz, seq, d = x.shape
    tm = TM_IN
    n_slots = 2
    row = lambda b, i: (b, i, 0)
    par = lambda b, i: (layer, 0, 0)
    vec = pl.BlockSpec((None, 1, d), par)
    head = pl.BlockSpec((None, 1, HEAD_DIM), par)
    out_attn = jax.ShapeDtypeStruct((bsz, seq, ATTN_WIDTH), BF16)
    out_d = jax.ShapeDtypeStruct((bsz, seq, d), BF16)
    return pl.pallas_call(
        _in_proj_kernel,
        grid=(bsz, seq // tm),
        in_specs=[pl.BlockSpec((None, tm, d), row),
                  pl.BlockSpec((None, None, 6, d), lambda b, i: (layer, b, 0, 0)),
                  vec,
                  pl.BlockSpec((None, d, IN_WIDTH), par, pipeline_mode=pl.Buffered(1)),
                  head, head, head, head,
                  pl.BlockSpec((None, tm, HEAD_DIM), row),
                  pl.BlockSpec((None, tm, HEAD_DIM), row),
                  pl.BlockSpec((None, CONV_K, d), par),
                  vec, vec, vec,
                  pl.BlockSpec((None, d, d), par, pipeline_mode=pl.Buffered(1))],
        out_specs=[pl.BlockSpec((None, tm, ATTN_WIDTH), row)] * 3 + [pl.BlockSpec((None, tm, d), row)] * 2,
        out_shape=[out_attn] * 3 + [out_d] * 2,
        scratch_shapes=[pltpu.VMEM((n_slots, tm, HEAD_DIM), F32), pltpu.VMEM((n_slots, tm, HEAD_DIM), F32),
                        pltpu.VMEM((tm + CONV_HALO, d), F32), pltpu.VMEM((tm, d), F32),
                        pltpu.VMEM((2, 7, tm + CONV_HALO - 8, HEAD_DIM), F32),
                        pltpu.VMEM((CONV_HALO, d), F32)],
        compiler_params=_params(2),
        name="in_proj",
    )(x, mod, g1, w_in, gq, gk, pgq, pgk, cos_tab, sin_tab, wdw, bdw, gln, bln, wco)


def _band_block(q, kc, vc, bias):
    s = lax.dot_general(q, kc, (((1,), (1,)), ((), ())), preferred_element_type=F32) + bias
    m = jnp.max(s, axis=-1, keepdims=True)
    p = jnp.exp2(s - m).astype(BF16)
    ones = jnp.ones((2 * BLOCK, HEAD_DIM), BF16)
    ol = jnp.dot(p, jnp.concatenate([vc, ones], axis=1), preferred_element_type=F32)
    return ol[:, :HEAD_DIM], jnp.broadcast_to(m, (BLOCK, HEAD_DIM)), ol[:, HEAD_DIM:]


def _attn_kernel(q0_ref, q1_ref, q2_ref, k0_ref, k1_ref, k2_ref, v0_ref, v1_ref, v2_ref,
                 k0h_ref, k1h_ref, k2h_ref, v0h_ref, v1h_ref, v2h_ref, o_ref,
                 a0_s, m0_s, l0_s, a1_s, m1_s, l1_s, a2_s, m2_s, l2_s, as_s, ms_s, ls_s, bias_s):
    first = pl.program_id(1) == 0
    qi = lax.broadcasted_iota(jnp.int32, (BLOCK, 2 * BLOCK), 0)
    kj = lax.broadcasted_iota(jnp.int32, (BLOCK, 2 * BLOCK), 1)
    band = (kj >= qi) & (kj <= qi + BLOCK)
    bias_s[0] = jnp.where(band, 0.0, NEG)
    bias_s[1] = jnp.where(band & (kj >= jnp.where(first, BLOCK, 0)), 0.0, NEG)
    inner, edge = 0, 1

    nblk = TILE_ATTN // BLOCK
    tin = TM_IN
    cat = functools.partial(jnp.concatenate, axis=0)

    def put(refs, idx, vals):
        for ref, val in zip(refs, vals):
            ref[idx, :] = val

    def rows(ref, n):
        return ref[n * BLOCK:(n + 1) * BLOCK, :]

    g0 = (a0_s, m0_s, l0_s)
    for n in range(nblk):
        if n == 0:
            kc, vc, which = cat([k0h_ref[...], rows(k0_ref, 0)]), cat([v0h_ref[...], rows(v0_ref, 0)]), edge
        else:
            win = slice((n - 1) * BLOCK, (n + 1) * BLOCK)
            kc, vc, which = k0_ref[win, :], v0_ref[win, :], inner
        put(g0, slice(n * BLOCK, (n + 1) * BLOCK), _band_block(rows(q0_ref, n), kc, vc, bias_s[which]))

    g1 = (a1_s, m1_s, l1_s)
    per_tile = tin // BLOCK
    for r in range(4):
        for n in range(TILE_ATTN // tin):
            cur = n * per_tile + r
            if n == 0:
                kp, vp, which = rows(k1h_ref, r), rows(v1h_ref, r), edge
            else:
                kp, vp, which = rows(k1_ref, cur - per_tile), rows(v1_ref, cur - per_tile), inner
            res = _band_block(rows(q1_ref, cur), cat([kp, rows(k1_ref, cur)]), cat([vp, rows(v1_ref, cur)]),
                              bias_s[which])
            put(g1, pl.ds(n * tin + r, BLOCK, stride=4), res)

    piece = tin // 16
    gs = (as_s, ms_s, ls_s)
    for r16 in range(16):
        pieces = lambda ref: [ref[c * tin + r16 * piece:c * tin + (r16 + 1) * piece, :]
                              for c in range(TILE_ATTN // tin)]
        res = _band_block(cat(pieces(q2_ref)), cat(pieces(k2h_ref) + pieces(k2_ref)),
                          cat(pieces(v2h_ref) + pieces(v2_ref)), bias_s[edge])
        put(gs, pl.ds((r16 % 4) * (TILE_ATTN // 4) + r16 // 4, BLOCK, stride=4), res)
    quarter = TILE_ATTN // 4
    for r in range(4):
        for dst, src in zip((a2_s, m2_s, l2_s), gs):
            dst[pl.ds(r, quarter, stride=4), :] = src[r * quarter:(r + 1) * quarter, :]

    ch = 256
    for c in range(TILE_ATTN // ch):
        rs = slice(c * ch, (c + 1) * ch)
        ma, mb, mc = m0_s[rs, :], m1_s[rs, :], m2_s[rs, :]
        mx = jnp.maximum(jnp.maximum(ma, mb), mc)
        ea, eb, ec = jnp.exp2(ma - mx), jnp.exp2(mb - mx), jnp.exp2(mc - mx)
        den = ea * l0_s[rs, :] + eb * l1_s[rs, :] + ec * l2_s[rs, :]
        num = ea * a0_s[rs, :] + eb * a1_s[rs, :] + ec * a2_s[rs, :]
        o_ref[rs, :] = (num * (1.0 / den)).astype(BF16)


def _attention(q, k, v):
    bsz, seq, _ = q.shape
    t = TILE_ATTN
    hpg = HEADS_PER_GROUP

    def cur(g):
        return pl.BlockSpec((None, t, HEAD_DIM), lambda b, i, j: (b, i, g * hpg + j))

    def halo(g, rows):
        per = t // rows
        return pl.BlockSpec((None, rows, HEAD_DIM),
                            lambda b, i, j: (b, jnp.maximum(i * per - 1, 0), g * hpg + j))

    halos = [halo(0, BLOCK), halo(1, TM_IN), halo(2, t)]
    scratch = [pltpu.VMEM((t, HEAD_DIM), F32)] * 12 + [pltpu.VMEM((2, BLOCK, 2 * BLOCK), F32)]
    return pl.pallas_call(
        _attn_kernel,
        grid=(bsz, seq // t, hpg),
        in_specs=[cur(0), cur(1), cur(2)] * 3 + halos * 2,
        out_specs=pl.BlockSpec((None, t, HEAD_DIM), lambda b, i, j: (b, i, j)),
        out_shape=jax.ShapeDtypeStruct((bsz, seq, ATTN_OUT), BF16),
        scratch_shapes=scratch,
        compiler_params=_params(3),
        name="attention",
    )(q, q, q, k, k, k, v, v, v, k, k, k, v, v, v)


def _mix_ffn_kernel(attn_ref, ga_ref, gyb_ref, x_ref, mod_ref, wap_ref, wo_ref, g2_ref, win_ref, wdw_ref, bdw_ref,
                    wdn_ref, o_ref, ext_s, carry_s, sh_s, act_s):
    tm = x_ref.shape[0]
    y_a = jnp.dot(attn_ref[...], wap_ref[...], preferred_element_type=F32)
    merged = ga_ref[...].astype(F32) * y_a + gyb_ref[...].astype(F32)
    x = x_ref[...] + mod_ref[2:3, :] * jnp.dot(merged.astype(BF16), wo_ref[...], preferred_element_type=F32)

    h = _rms_mod(x, g2_ref[...], mod_ref[3:4, :], mod_ref[4:5, :]).astype(BF16)

    @pl.when(pl.program_id(1) == 0)
    def _():
        carry_s[...] = jnp.zeros_like(carry_s)

    cw = FFN_CHUNK
    for c0 in range(0, D_FF, cw):
        w = min(cw, D_FF - c0)
        cols = slice(c0, c0 + w)
        g = jnp.dot(h, win_ref[:, c0:c0 + w], preferred_element_type=F32)
        up = jnp.dot(h, win_ref[:, D_FF + c0:D_FF + c0 + w], preferred_element_type=F32)
        ext_s[0:FFN_HALO, 0:w] = carry_s[:, cols]
        ext_s[FFN_HALO:, 0:w] = g
        carry_s[:, cols] = g[tm - FFN_HALO:, :]
        gc = bdw_ref[:, cols] + wdw_ref[2:3, cols] * g
        for kk in range(FFN_K - 1):
            start = FFN_HALO - (FFN_K - 1) + kk
            sh_s[kk, :, 0:w] = ext_s[start:start + tm, 0:w]
            gc = gc + wdw_ref[kk:kk + 1, cols] * sh_s[kk, :, 0:w]
        act_s[:, cols] = (gc * _sigmoid(gc) * up).astype(BF16)
    out = jnp.dot(act_s[...], wdn_ref[...], preferred_element_type=F32)
    o_ref[...] = x + mod_ref[5:6, :] * out


def _mix_ffn(attn, ga, gyb, x, mod, wap, wo, g2, win, wdw, bdw, wdn, layer):
    bsz, seq, d = x.shape
    tm = TM_FFN
    row = lambda b, i: (b, i, 0)
    par = lambda b, i: (layer, 0, 0)
    once = dict(pipeline_mode=pl.Buffered(1))
    return pl.pallas_call(
        _mix_ffn_kernel,
        grid=(bsz, seq // tm),
        in_specs=[pl.BlockSpec((None, tm, ATTN_OUT), row),
                  pl.BlockSpec((None, tm, d), row),
                  pl.BlockSpec((None, tm, d), row),
                  pl.BlockSpec((None, tm, d), row),
                  pl.BlockSpec((None, None, 6, d), lambda b, i: (layer, b, 0, 0)),
                  pl.BlockSpec((None, ATTN_OUT, d), par, **once),
                  pl.BlockSpec((None, d, d), par, **once),
                  pl.BlockSpec((None, 1, d), par),
                  pl.BlockSpec((None, d, 2 * D_FF), par, **once),
                  pl.BlockSpec((None, FFN_K, D_FF), par),
                  pl.BlockSpec((None, 1, D_FF), par),
                  pl.BlockSpec((None, D_FF, d), par, **once)],
        out_specs=pl.BlockSpec((None, tm, d), row),
        out_shape=jax.ShapeDtypeStruct((bsz, seq, d), F32),
        scratch_shapes=[pltpu.VMEM((tm + FFN_HALO, FFN_CHUNK), F32), pltpu.VMEM((FFN_HALO, D_FF), F32),
                        pltpu.VMEM((FFN_K - 1, tm, FFN_CHUNK), F32), pltpu.VMEM((tm, D_FF), BF16)],
        compiler_params=_params(2),
        name="mix_ffn",
    )(attn, ga, gyb, x, mod, wap, wo, g2, win, wdw, bdw, wdn)


def _rope_tables(positions):
    inv_freq = ROPE_THETA ** (-jnp.arange(0, ROT_DIM, 2, dtype=F32) / ROT_DIM)
    ang = positions.astype(F32)[..., None] * inv_freq
    cos, sin = jnp.cos(ang), jnp.sin(ang)
    gap = ROT_SHIFT - ROT_HALF
    ones = jnp.ones(cos.shape[:-1] + (gap,), F32)
    zeros = jnp.zeros(cos.shape[:-1] + (gap,), F32)
    cos_tab = jnp.concatenate([cos, ones, cos, ones], axis=-1)
    sin_tab = jnp.concatenate([-sin, zeros, sin, zeros], axis=-1)
    return cos_tab, sin_tab


def kernel(x, c, positions, w_ada, b_ada, g_norm1, w_in, g_q, g_k, w_attn_proj, w_conv_dw, b_conv_dw,
           g_conv_ln, b_conv_ln, w_conv_out, w_o, g_norm2, w_ffn_in, w_ffn_dw, b_ffn_dw, w_ffn_down):
    depth = w_ada.shape[0]
    bsz, seq, d = x.shape
    assert (d, seq % TILE_ATTN) == (D_MODEL, 0)

    mod = _ada(c, w_ada, b_ada).reshape(depth, bsz, 6, d)
    cos_tab, sin_tab = _rope_tables(positions)
    vec = lambda a: a.reshape(depth, 1, a.shape[-1])
    perm = jnp.asarray(HEAD_PERM)
    w_qk = w_in[:, :, :2 * ATTN_WIDTH].reshape(depth, d, 2 * N_HEADS, HEAD_DIM)[..., perm]
    w_in_b = jnp.concatenate([w_qk.reshape(depth, d, 2 * ATTN_WIDTH).astype(BF16),
                              w_in[:, :, 2 * ATTN_WIDTH:].astype(BF16)], axis=-1)
    g_q, g_k = g_q[:, perm], g_k[:, perm]
    wap_b, wco_b, wo_b = (w.astype(BF16) for w in (w_attn_proj, w_conv_out, w_o))
    win_b, wdn_b = w_ffn_in.astype(BF16), w_ffn_down.astype(BF16)
    g1, g2, gq, gk = vec(g_norm1), vec(g_norm2), vec(g_q), vec(g_k)
    pgq, pgk = vec(jnp.roll(g_q, ROT_SHIFT, axis=-1)), vec(jnp.roll(g_k, ROT_SHIFT, axis=-1))
    bdw, gln, bln, bfdw = vec(b_conv_dw), vec(g_conv_ln), vec(b_conv_ln), vec(b_ffn_dw)

    for layer in range(depth):
        q, k, v, ga, gyb = _in_proj(x, mod, g1, w_in_b, gq, gk, pgq, pgk, cos_tab, sin_tab,
                                    w_conv_dw, bdw, gln, bln, wco_b, layer)
        attn = _attention(q, k, v)
        x = _mix_ffn(attn, ga, gyb, x, mod, wap_b, wo_b, g2, win_b, w_ffn_dw, bfdw, wdn_b, layer)
    return x
```

```python
import functools

import jax
import jax.numpy as jnp
from jax import lax
from jax.experimental import pallas as pl
from jax.experimental.pallas import tpu as pltpu

F32 = jnp.float32
BF16 = jnp.bfloat16

D_MODEL = 1024
HEAD_DIM = 128
DILATIONS = (1, 4, 16)
HEADS_PER_GROUP = 4
N_HEADS = HEADS_PER_GROUP * len(DILATIONS)
ATTN_WIDTH = N_HEADS * HEAD_DIM
ATTN_OUT = HEADS_PER_GROUP * HEAD_DIM
BLOCK = 128
ROT_DIM = HEAD_DIM // 4
ROT_HALF = ROT_DIM // 2
ROT_SHIFT = HEAD_DIM // 2
HEAD_PERM = (tuple(range(ROT_HALF)) + tuple(range(ROT_DIM, ROT_SHIFT + ROT_HALF))
             + tuple(range(ROT_HALF, ROT_DIM)) + tuple(range(ROT_SHIFT + ROT_HALF, HEAD_DIM)))
ROPE_THETA = 500000.0
CONV_K = 31
D_FF = 2816
FFN_K = 3
IN_WIDTH = 3 * ATTN_WIDTH + 4 * D_MODEL
EPS = 1e-6
NEG = -1e30
Q_SCALE = HEAD_DIM ** -0.5 * 1.4426950408889634

TM_IN = 512
TILE_ATTN = 2048
TM_FFN = 512
CONV_HALO = 32
FFN_HALO = 8
FFN_CHUNK = 768
V7X_VMEM_LIMIT = 56 * 1024 * 1024


def _params(n_axes):
    return pltpu.CompilerParams(dimension_semantics=("arbitrary",) * n_axes,
                                vmem_limit_bytes=V7X_VMEM_LIMIT)


def _sigmoid(x):
    return 1.0 / (1.0 + jnp.exp(-x))


def _ada_kernel(c_ref, w_ref, b_ref, o_ref):
    c = c_ref[...]
    c_act = c * _sigmoid(c)
    o_ref[...] = jnp.dot(c_act, w_ref[...], preferred_element_type=F32) + b_ref[...]


def _ada(c, w_ada, b_ada):
    depth, d, n = w_ada.shape
    bsz = c.shape[0]
    nt = 1024
    return pl.pallas_call(
        _ada_kernel,
        grid=(depth, n // nt),
        in_specs=[pl.BlockSpec((bsz, d), lambda l, j: (0, 0)),
                  pl.BlockSpec((None, d, nt), lambda l, j: (l, 0, j)),
                  pl.BlockSpec((None, 1, nt), lambda l, j: (l, 0, j))],
        out_specs=pl.BlockSpec((None, bsz, nt), lambda l, j: (l, 0, j)),
        out_shape=jax.ShapeDtypeStruct((depth, bsz, n), F32),
        compiler_params=_params(2),
        name="ada",
    )(c, w_ada, b_ada.reshape(depth, 1, n))


def _rms_mod(x, g, shift, scale):
    ms = jnp.mean(x * x, axis=-1, keepdims=True)
    h = x * lax.rsqrt(ms + EPS) * g
    return h * (1.0 + scale) + shift


def _dwconv31_chunk(c, ext_s, sh_s, y_s, wdw_ref, bdw_ref, tm):
    rc = 64
    lead = CONV_HALO - (CONV_K - 1)
    rows_sh = sh_s.shape[2]
    cols = slice(c * HEAD_DIM, (c + 1) * HEAD_DIM)
    sh = sh_s.at[c % sh_s.shape[0]]
    for s in range(1, 8):
        sh[s - 1, :, :] = ext_s[s:s + rows_sh, cols]
    taps = [jnp.broadcast_to(wdw_ref[kk:kk + 1, cols], (rc, HEAD_DIM)) for kk in range(CONV_K)]
    bias = jnp.broadcast_to(bdw_ref[:, cols], (rc, HEAD_DIM))
    for i in range(tm // rc):
        acc = bias
        for kk in range(CONV_K):
            r0 = i * rc + (lead + kk) // 8 * 8
            s = (lead + kk) % 8
            win = ext_s[r0:r0 + rc, cols] if s == 0 else sh[s - 1, r0:r0 + rc, :]
            acc = acc + taps[kk] * win
        y_s[i * rc:(i + 1) * rc, cols] = acc


def _in_proj_kernel(x_ref, mod_ref, g1_ref, wqk_ref, w_ref, gq_ref, gk_ref, pgq_ref, pgk_ref, cos_ref, sin_ref,
                    wdw_ref, bdw_ref, gln_ref, bln_ref, wco_ref,
                    q_ref, k_ref, v_ref, ga_ref, gyb_ref,
                    zs_ref, zs2_ref, ext_s, y_s, sh_s, carry_s):
    tm = x_ref.shape[0]
    h = _rms_mod(x_ref[...], g1_ref[...], mod_ref[0:1, :], mod_ref[1:2, :]).astype(BF16)

    @pl.when(pl.program_id(1) == 0)
    def _():
        carry_s[...] = jnp.zeros_like(carry_s)

    cw = 512
    base_val = ATTN_WIDTH
    base_gate = base_val + D_MODEL
    ext_s[0:CONV_HALO, :] = carry_s[...]
    for cc in range(D_MODEL // cw):
        val = jnp.dot(h, w_ref[:, base_val + cc * cw:base_val + (cc + 1) * cw], preferred_element_type=F32)
        gate = jnp.dot(h, w_ref[:, base_gate + cc * cw:base_gate + (cc + 1) * cw], preferred_element_type=F32)
        ext_s[CONV_HALO:, cc * cw:(cc + 1) * cw] = val * _sigmoid(gate)
    carry_s[...] = ext_s[tm:tm + CONV_HALO, :]
    conv_chunks = list(range(D_MODEL // HEAD_DIM))

    cos = cos_ref[...]
    sin = sin_ref[...]

    def store_head(o_ref, col, val, dil, slot):
        cols = slice(col, col + HEAD_DIM)
        if dil == 1:
            o_ref[:, cols] = val.astype(BF16)
            return
        zs = zs_ref.at[slot]
        zs[...] = val
        per = tm // 4
        if dil == 4:
            for r in range(4):
                o_ref[r * per:(r + 1) * per, cols] = zs[pl.ds(r, per, stride=4), :].astype(BF16)
            return
        zs2 = zs2_ref.at[slot]
        for r in range(4):
            zs2[r * per:(r + 1) * per, :] = zs[pl.ds(r, per, stride=4), :]
        per2 = per // 4
        for r in range(4):
            for rp in range(4):
                r16 = r + 4 * rp
                o_ref[r16 * per2:(r16 + 1) * per2, cols] = (
                    zs2[pl.ds(r * per + rp, per2, stride=4), :].astype(BF16))

    def norm_rope(t, a_tab, b_tab, post):
        ms = jnp.mean(t * t, axis=-1, keepdims=True)
        rinv = lax.rsqrt(ms + EPS) * post
        return (t * a_tab + pltpu.roll(t, ROT_SHIFT, 1) * b_tab) * rinv

    slot = 0
    for which, (o_ref, g_ref, pg_ref, post) in enumerate(
            ((q_ref, gq_ref, pgq_ref, Q_SCALE), (k_ref, gk_ref, pgk_ref, 1.0), (v_ref, None, None, None))):
        if g_ref is not None:
            a_tab = cos * g_ref[...]
            b_tab = sin * pg_ref[...]
        for gi, dil in enumerate(DILATIONS):
            if g_ref is not None:
                base = which * ATTN_WIDTH + gi * ATTN_OUT
                z = jnp.dot(h, wqk_ref[:, base:base + ATTN_OUT], preferred_element_type=F32)
            else:
                z = jnp.dot(h, w_ref[:, gi * ATTN_OUT:(gi + 1) * ATTN_OUT], preferred_element_type=F32)
            for hh in range(HEADS_PER_GROUP):
                t = z[:, hh * HEAD_DIM:(hh + 1) * HEAD_DIM]
                if g_ref is not None:
                    t = norm_rope(t, a_tab, b_tab, post)
                store_head(o_ref, gi * ATTN_OUT + hh * HEAD_DIM, t, dil, slot % zs_ref.shape[0])
                slot += 1
            if conv_chunks:
                _dwconv31_chunk(conv_chunks.pop(0), ext_s, sh_s, y_s, wdw_ref, bdw_ref, tm)

    y = y_s[...]
    mu = jnp.mean(y, axis=-1, keepdims=True)
    yc = y - mu
    var = jnp.mean(yc * yc, axis=-1, keepdims=True)
    yn = yc * lax.rsqrt(var + EPS) * gln_ref[...] + bln_ref[...]
    ub = (yn * _sigmoid(yn)).astype(BF16)
    base_a = base_gate + D_MODEL
    base_b = base_a + D_MODEL
    for cc in range(D_MODEL // cw):
        cols = slice(cc * cw, (cc + 1) * cw)
        y_b = jnp.dot(ub, wco_ref[:, cols], preferred_element_type=F32)
        zb = jnp.dot(h, w_ref[:, base_b + cc * cw:base_b + (cc + 1) * cw], preferred_element_type=F32)
        gyb_ref[:, cols] = (_sigmoid(zb) * y_b).astype(BF16)
        za = jnp.dot(h, w_ref[:, base_a + cc * cw:base_a + (cc + 1) * cw], preferred_element_type=F32)
        ga_ref[:, cols] = _sigmoid(za).astype(BF16)


def _in_proj(x, mod, g1, w_qk, w_rest, gq, gk, pgq, pgk, cos_tab, sin_tab, wdw, bdw, gln, bln, wco, layer):
    bsz, seq, d = x.shape
    tm = TM_IN
    n_slots = 2
    row = lambda b, i: (b, i, 0)
    par = lambda b, i: (layer, 0, 0)
    vec = pl.BlockSpec((None, 1, d), par)
    head = pl.BlockSpec((None, 1, HEAD_DIM), par)
    out_attn = jax.ShapeDtypeStruct((bsz, seq, ATTN_WIDTH), BF16)
    out_d = jax.ShapeDtypeStruct((bsz, seq, d), BF16)
    return pl.pallas_call(
        _in_proj_kernel,
        grid=(bsz, seq // tm),
        in_specs=[pl.BlockSpec((None, tm, d), row),
                  pl.BlockSpec((None, None, 6, d), lambda b, i: (layer, b, 0, 0)),
                  vec,
                  pl.BlockSpec((None, d, 2 * ATTN_WIDTH), par, pipeline_mode=pl.Buffered(1)),
                  pl.BlockSpec((None, d, IN_WIDTH - 2 * ATTN_WIDTH), par, pipeline_mode=pl.Buffered(1)),
                  head, head, head, head,
                  pl.BlockSpec((None, tm, HEAD_DIM), row),
                  pl.BlockSpec((None, tm, HEAD_DIM), row),
                  pl.BlockSpec((None, CONV_K, d), par),
                  vec, vec, vec,
                  pl.BlockSpec((None, d, d), par, pipeline_mode=pl.Buffered(1))],
        out_specs=[pl.BlockSpec((None, tm, ATTN_WIDTH), row)] * 3 + [pl.BlockSpec((None, tm, d), row)] * 2,
        out_shape=[out_attn] * 3 + [out_d] * 2,
        scratch_shapes=[pltpu.VMEM((n_slots, tm, HEAD_DIM), F32), pltpu.VMEM((n_slots, tm, HEAD_DIM), F32),
                        pltpu.VMEM((tm + CONV_HALO, d), F32), pltpu.VMEM((tm, d), F32),
                        pltpu.VMEM((2, 7, tm + CONV_HALO - 8, HEAD_DIM), F32),
                        pltpu.VMEM((CONV_HALO, d), F32)],
        compiler_params=_params(2),
        name="in_proj",
    )(x, mod, g1, w_qk, w_rest, gq, gk, pgq, pgk, cos_tab, sin_tab, wdw, bdw, gln, bln, wco)


def _band_block(q, kc, vc, bias):
    s = lax.dot_general(q, kc, (((1,), (1,)), ((), ())), preferred_element_type=F32) + bias
    m = jnp.max(s, axis=-1, keepdims=True)
    p = jnp.exp2(s - m).astype(BF16)
    ones = jnp.ones((2 * BLOCK, HEAD_DIM), BF16)
    ol = jnp.dot(p, jnp.concatenate([vc, ones], axis=1), preferred_element_type=F32)
    return ol[:, :HEAD_DIM], jnp.broadcast_to(m, (BLOCK, HEAD_DIM)), ol[:, HEAD_DIM:]


def _attn_kernel(q0_ref, q1_ref, q2_ref, k0_ref, k1_ref, k2_ref, v0_ref, v1_ref, v2_ref,
                 k0h_ref, k1h_ref, k2h_ref, v0h_ref, v1h_ref, v2h_ref, o_ref,
                 a0_s, m0_s, l0_s, a1_s, m1_s, l1_s, a2_s, m2_s, l2_s, as_s, ms_s, ls_s, bias_s):
    first = pl.program_id(1) == 0
    qi = lax.broadcasted_iota(jnp.int32, (BLOCK, 2 * BLOCK), 0)
    kj = lax.broadcasted_iota(jnp.int32, (BLOCK, 2 * BLOCK), 1)
    band = (kj >= qi) & (kj <= qi + BLOCK)
    bias_s[0] = jnp.where(band, 0.0, NEG)
    bias_s[1] = jnp.where(band & (kj >= jnp.where(first, BLOCK, 0)), 0.0, NEG)
    inner, edge = 0, 1

    nblk = TILE_ATTN // BLOCK
    tin = TM_IN
    cat = functools.partial(jnp.concatenate, axis=0)

    def put(refs, idx, vals):
        for ref, val in zip(refs, vals):
            ref[idx, :] = val

    def rows(ref, n):
        return ref[n * BLOCK:(n + 1) * BLOCK, :]

    g0 = (a0_s, m0_s, l0_s)
    for n in range(nblk):
        if n == 0:
            kc, vc, which = cat([k0h_ref[...], rows(k0_ref, 0)]), cat([v0h_ref[...], rows(v0_ref, 0)]), edge
        else:
            win = slice((n - 1) * BLOCK, (n + 1) * BLOCK)
            kc, vc, which = k0_ref[win, :], v0_ref[win, :], inner
        put(g0, slice(n * BLOCK, (n + 1) * BLOCK), _band_block(rows(q0_ref, n), kc, vc, bias_s[which]))

    g1 = (a1_s, m1_s, l1_s)
    per_tile = tin // BLOCK
    for r in range(4):
        for n in range(TILE_ATTN // tin):
            cur = n * per_tile + r
            if n == 0:
                kp, vp, which = rows(k1h_ref, r), rows(v1h_ref, r), edge
            else:
                kp, vp, which = rows(k1_ref, cur - per_tile), rows(v1_ref, cur - per_tile), inner
            res = _band_block(rows(q1_ref, cur), cat([kp, rows(k1_ref, cur)]), cat([vp, rows(v1_ref, cur)]),
                              bias_s[which])
            put(g1, pl.ds(n * tin + r, BLOCK, stride=4), res)

    piece = tin // 16
    gs = (as_s, ms_s, ls_s)
    for r16 in range(16):
        pieces = lambda ref: [ref[c * tin + r16 * piece:c * tin + (r16 + 1) * piece, :]
                              for c in range(TILE_ATTN // tin)]
        res = _band_block(cat(pieces(q2_ref)), cat(pieces(k2h_ref) + pieces(k2_ref)),
                          cat(pieces(v2h_ref) + pieces(v2_ref)), bias_s[edge])
        put(gs, pl.ds((r16 % 4) * (TILE_ATTN // 4) + r16 // 4, BLOCK, stride=4), res)
    quarter = TILE_ATTN // 4
    for r in range(4):
        for dst, src in zip((a2_s, m2_s, l2_s), gs):
            dst[pl.ds(r, quarter, stride=4), :] = src[r * quarter:(r + 1) * quarter, :]

    ch = 256
    for c in range(TILE_ATTN // ch):
        rs = slice(c * ch, (c + 1) * ch)
        ma, mb, mc = m0_s[rs, :], m1_s[rs, :], m2_s[rs, :]
        mx = jnp.maximum(jnp.maximum(ma, mb), mc)
        ea, eb, ec = jnp.exp2(ma - mx), jnp.exp2(mb - mx), jnp.exp2(mc - mx)
        den = ea * l0_s[rs, :] + eb * l1_s[rs, :] + ec * l2_s[rs, :]
        num = ea * a0_s[rs, :] + eb * a1_s[rs, :] + ec * a2_s[rs, :]
        o_ref[rs, :] = (num * (1.0 / den)).astype(BF16)


def _attention(q, k, v):
    bsz, seq, _ = q.shape
    t = TILE_ATTN
    hpg = HEADS_PER_GROUP

    def cur(g):
        return pl.BlockSpec((None, t, HEAD_DIM), lambda b, i, j: (b, i, g * hpg + j))

    def halo(g, rows):
        per = t // rows
        return pl.BlockSpec((None, rows, HEAD_DIM),
                            lambda b, i, j: (b, jnp.maximum(i * per - 1, 0), g * hpg + j))

    halos = [halo(0, BLOCK), halo(1, TM_IN), halo(2, t)]
    scratch = [pltpu.VMEM((t, HEAD_DIM), F32)] * 12 + [pltpu.VMEM((2, BLOCK, 2 * BLOCK), F32)]
    return pl.pallas_call(
        _attn_kernel,
        grid=(bsz, seq // t, hpg),
        in_specs=[cur(0), cur(1), cur(2)] * 3 + halos * 2,
        out_specs=pl.BlockSpec((None, t, HEAD_DIM), lambda b, i, j: (b, i, j)),
        out_shape=jax.ShapeDtypeStruct((bsz, seq, ATTN_OUT), BF16),
        scratch_shapes=scratch,
        compiler_params=_params(3),
        name="attention",
    )(q, q, q, k, k, k, v, v, v, k, k, k, v, v, v)


def _mix_ffn_kernel(attn_ref, ga_ref, gyb_ref, x_ref, mod_ref, wap_ref, wo_ref, g2_ref, win_ref, wdw_ref, bdw_ref,
                    wdn_ref, o_ref, ext_s, carry_s, sh_s, act_s):
    tm = x_ref.shape[0]
    y_a = jnp.dot(attn_ref[...], wap_ref[...], preferred_element_type=F32)
    merged = ga_ref[...].astype(F32) * y_a + gyb_ref[...].astype(F32)
    x = x_ref[...] + mod_ref[2:3, :] * jnp.dot(merged.astype(BF16), wo_ref[...], preferred_element_type=F32)

    h = _rms_mod(x, g2_ref[...], mod_ref[3:4, :], mod_ref[4:5, :]).astype(BF16)

    @pl.when(pl.program_id(1) == 0)
    def _():
        carry_s[...] = jnp.zeros_like(carry_s)

    cw = FFN_CHUNK
    for c0 in range(0, D_FF, cw):
        w = min(cw, D_FF - c0)
        cols = slice(c0, c0 + w)
        g = jnp.dot(h, win_ref[:, c0:c0 + w], preferred_element_type=F32)
        up = jnp.dot(h, win_ref[:, D_FF + c0:D_FF + c0 + w], preferred_element_type=F32)
        ext_s[0:FFN_HALO, 0:w] = carry_s[:, cols]
        ext_s[FFN_HALO:, 0:w] = g
        carry_s[:, cols] = g[tm - FFN_HALO:, :]
        gc = bdw_ref[:, cols] + wdw_ref[2:3, cols] * g
        for kk in range(FFN_K - 1):
            start = FFN_HALO - (FFN_K - 1) + kk
            sh_s[kk, :, 0:w] = ext_s[start:start + tm, 0:w]
            gc = gc + wdw_ref[kk:kk + 1, cols] * sh_s[kk, :, 0:w]
        act_s[:, cols] = (gc * _sigmoid(gc) * up).astype(BF16)
    out = jnp.dot(act_s[...], wdn_ref[...], preferred_element_type=F32)
    o_ref[...] = x + mod_ref[5:6, :] * out


def _mix_ffn(attn, ga, gyb, x, mod, wap, wo, g2, win, wdw, bdw, wdn, layer):
    bsz, seq, d = x.shape
    tm = TM_FFN
    row = lambda b, i: (b, i, 0)
    par = lambda b, i: (layer, 0, 0)
    once = dict(pipeline_mode=pl.Buffered(1))
    return pl.pallas_call(
        _mix_ffn_kernel,
        grid=(bsz, seq // tm),
        in_specs=[pl.BlockSpec((None, tm, ATTN_OUT), row),
                  pl.BlockSpec((None, tm, d), row),
                  pl.BlockSpec((None, tm, d), row),
                  pl.BlockSpec((None, tm, d), row),
                  pl.BlockSpec((None, None, 6, d), lambda b, i: (layer, b, 0, 0)),
                  pl.BlockSpec((None, ATTN_OUT, d), par, **once),
                  pl.BlockSpec((None, d, d), par, **once),
                  pl.BlockSpec((None, 1, d), par),
                  pl.BlockSpec((None, d, 2 * D_FF), par, **once),
                  pl.BlockSpec((None, FFN_K, D_FF), par),
                  pl.BlockSpec((None, 1, D_FF), par),
                  pl.BlockSpec((None, D_FF, d), par, **once)],
        out_specs=pl.BlockSpec((None, tm, d), row),
        out_shape=jax.ShapeDtypeStruct((bsz, seq, d), F32),
        scratch_shapes=[pltpu.VMEM((tm + FFN_HALO, FFN_CHUNK), F32), pltpu.VMEM((FFN_HALO, D_FF), F32),
                        pltpu.VMEM((FFN_K - 1, tm, FFN_CHUNK), F32), pltpu.VMEM((tm, D_FF), BF16)],
        compiler_params=_params(2),
        name="mix_ffn",
    )(attn, ga, gyb, x, mod, wap, wo, g2, win, wdw, bdw, wdn)


def _rope_tables(positions):
    inv_freq = ROPE_THETA ** (-jnp.arange(0, ROT_DIM, 2, dtype=F32) / ROT_DIM)
    ang = positions.astype(F32)[..., None] * inv_freq
    cos, sin = jnp.cos(ang), jnp.sin(ang)
    gap = ROT_SHIFT - ROT_HALF
    ones = jnp.ones(cos.shape[:-1] + (gap,), F32)
    zeros = jnp.zeros(cos.shape[:-1] + (gap,), F32)
    cos_tab = jnp.concatenate([cos, ones, cos, ones], axis=-1)
    sin_tab = jnp.concatenate([-sin, zeros, sin, zeros], axis=-1)
    return cos_tab, sin_tab


def kernel(x, c, positions, w_ada, b_ada, g_norm1, w_in, g_q, g_k, w_attn_proj, w_conv_dw, b_conv_dw,
           g_conv_ln, b_conv_ln, w_conv_out, w_o, g_norm2, w_ffn_in, w_ffn_dw, b_ffn_dw, w_ffn_down):
    depth = w_ada.shape[0]
    bsz, seq, d = x.shape
    assert (d, seq % TILE_ATTN) == (D_MODEL, 0)

    mod = _ada(c, w_ada, b_ada).reshape(depth, bsz, 6, d)
    cos_tab, sin_tab = _rope_tables(positions)
    vec = lambda a: a.reshape(depth, 1, a.shape[-1])
    perm = jnp.asarray(HEAD_PERM)
    perm_mat = jnp.eye(HEAD_DIM, dtype=F32)[:, perm]
    w_qk = jnp.einsum("ldhk,kj->ldhj", w_in[:, :, :2 * ATTN_WIDTH].reshape(depth, d, 2 * N_HEADS, HEAD_DIM),
                      perm_mat, preferred_element_type=BF16).reshape(depth, d, 2 * ATTN_WIDTH)
    w_rest = w_in[:, :, 2 * ATTN_WIDTH:].astype(BF16)
    g_q, g_k = g_q[:, perm], g_k[:, perm]
    wap_b, wco_b, wo_b = (w.astype(BF16) for w in (w_attn_proj, w_conv_out, w_o))
    win_b, wdn_b = w_ffn_in.astype(BF16), w_ffn_down.astype(BF16)
    g1, g2, gq, gk = vec(g_norm1), vec(g_norm2), vec(g_q), vec(g_k)
    pgq, pgk = vec(jnp.roll(g_q, ROT_SHIFT, axis=-1)), vec(jnp.roll(g_k, ROT_SHIFT, axis=-1))
    bdw, gln, bln, bfdw = vec(b_conv_dw), vec(g_conv_ln), vec(b_conv_ln), vec(b_ffn_dw)

    for layer in range(depth):
        q, k, v, ga, gyb = _in_proj(x, mod, g1, w_qk, w_rest, gq, gk, pgq, pgk, cos_tab, sin_tab,
                                    w_conv_dw, bdw, gln, bln, wco_b, layer)
        attn = _attention(q, k, v)
        x = _mix_ffn(attn, ga, gyb, x, mod, wap_b, wo_b, g2, win_b, w_ffn_dw, bfdw, wdn_b, layer)
    return x
```

```python
import functools

import jax
import jax.numpy as jnp
from jax import lax
from jax.experimental import pallas as pl
from jax.experimental.pallas import tpu as pltpu

F32 = jnp.float32
BF16 = jnp.bfloat16

D_MODEL = 1024
HEAD_DIM = 128
DILATIONS = (1, 4, 16)
HEADS_PER_GROUP = 4
N_HEADS = HEADS_PER_GROUP * len(DILATIONS)
ATTN_WIDTH = N_HEADS * HEAD_DIM
ATTN_OUT = HEADS_PER_GROUP * HEAD_DIM
BLOCK = 128
ROT_DIM = HEAD_DIM // 4
ROT_HALF = ROT_DIM // 2
ROT_SHIFT = HEAD_DIM // 2
HEAD_PERM = (tuple(range(ROT_HALF)) + tuple(range(ROT_DIM, ROT_SHIFT + ROT_HALF))
             + tuple(range(ROT_HALF, ROT_DIM)) + tuple(range(ROT_SHIFT + ROT_HALF, HEAD_DIM)))
ROPE_THETA = 500000.0
CONV_K = 31
D_FF = 2816
FFN_K = 3
IN_WIDTH = 3 * ATTN_WIDTH + 4 * D_MODEL
EPS = 1e-6
NEG = -1e30
Q_SCALE = HEAD_DIM ** -0.5 * 1.4426950408889634

TM_IN = 512
TILE_ATTN = 2048
TM_FFN = 512
CONV_HALO = 32
FFN_HALO = 8
FFN_CHUNK = 768
V7X_VMEM_LIMIT = 56 * 1024 * 1024


def _params(n_axes):
    return pltpu.CompilerParams(dimension_semantics=("arbitrary",) * n_axes,
                                vmem_limit_bytes=V7X_VMEM_LIMIT)


def _sigmoid(x):
    return 1.0 / (1.0 + jnp.exp(-x))


def _ada_kernel(c_ref, w_ref, b_ref, o_ref):
    c = c_ref[...]
    c_act = c * _sigmoid(c)
    o_ref[...] = jnp.dot(c_act, w_ref[...], preferred_element_type=F32) + b_ref[...]


def _ada(c, w_ada, b_ada):
    depth, d, n = w_ada.shape
    bsz = c.shape[0]
    nt = 1024
    return pl.pallas_call(
        _ada_kernel,
        grid=(depth, n // nt),
        in_specs=[pl.BlockSpec((bsz, d), lambda l, j: (0, 0)),
                  pl.BlockSpec((None, d, nt), lambda l, j: (l, 0, j)),
                  pl.BlockSpec((None, 1, nt), lambda l, j: (l, 0, j))],
        out_specs=pl.BlockSpec((None, bsz, nt), lambda l, j: (l, 0, j)),
        out_shape=jax.ShapeDtypeStruct((depth, bsz, n), F32),
        compiler_params=_params(2),
        name="ada",
    )(c, w_ada, b_ada.reshape(depth, 1, n))


def _rms_mod(x, g, shift, scale):
    ms = jnp.mean(x * x, axis=-1, keepdims=True)
    h = x * lax.rsqrt(ms + EPS) * g
    return h * (1.0 + scale) + shift


def _dwconv31_chunk(c, ext_s, sh_s, y_s, wdw_ref, bdw_ref, tm):
    rc = 64
    lead = CONV_HALO - (CONV_K - 1)
    rows_sh = sh_s.shape[2]
    cols = slice(c * HEAD_DIM, (c + 1) * HEAD_DIM)
    sh = sh_s.at[c % sh_s.shape[0]]
    for s in range(1, 8):
        sh[s - 1, :, :] = ext_s[s:s + rows_sh, cols]
    taps = [jnp.broadcast_to(wdw_ref[kk:kk + 1, cols], (rc, HEAD_DIM)) for kk in range(CONV_K)]
    bias = jnp.broadcast_to(bdw_ref[:, cols], (rc, HEAD_DIM))
    for i in range(tm // rc):
        acc = bias
        for kk in range(CONV_K):
            r0 = i * rc + (lead + kk) // 8 * 8
            s = (lead + kk) % 8
            win = ext_s[r0:r0 + rc, cols] if s == 0 else sh[s - 1, r0:r0 + rc, :]
            acc = acc + taps[kk] * win
        y_s[i * rc:(i + 1) * rc, cols] = acc


def _in_proj_kernel(x_ref, mod_ref, g1_ref, wqk_ref, w_ref, gq_ref, gk_ref, pgq_ref, pgk_ref, cos_ref, sin_ref,
                    wdw_ref, bdw_ref, gln_ref, bln_ref, wco_ref,
                    q_ref, k_ref, v_ref, ga_ref, gyb_ref,
                    zs_ref, zs2_ref, ext_s, y_s, sh_s, carry_s):
    tm = x_ref.shape[0]
    h = _rms_mod(x_ref[...], g1_ref[...], mod_ref[0:1, :], mod_ref[1:2, :]).astype(BF16)

    @pl.when(pl.program_id(1) == 0)
    def _():
        carry_s[...] = jnp.zeros_like(carry_s)

    cw = 512
    base_val = ATTN_WIDTH
    base_gate = base_val + D_MODEL
    ext_s[0:CONV_HALO, :] = carry_s[...]
    for cc in range(D_MODEL // cw):
        val = jnp.dot(h, w_ref[:, base_val + cc * cw:base_val + (cc + 1) * cw], preferred_element_type=F32)
        gate = jnp.dot(h, w_ref[:, base_gate + cc * cw:base_gate + (cc + 1) * cw], preferred_element_type=F32)
        ext_s[CONV_HALO:, cc * cw:(cc + 1) * cw] = val * _sigmoid(gate)
    carry_s[...] = ext_s[tm:tm + CONV_HALO, :]
    conv_chunks = list(range(D_MODEL // HEAD_DIM))

    cos = cos_ref[...]
    sin = sin_ref[...]

    def store_head(o_ref, col, val, dil, slot):
        cols = slice(col, col + HEAD_DIM)
        if dil == 1:
            o_ref[:, cols] = val.astype(BF16)
            return
        zs = zs_ref.at[slot]
        zs[...] = val
        per = tm // 4
        if dil == 4:
            for r in range(4):
                o_ref[r * per:(r + 1) * per, cols] = zs[pl.ds(r, per, stride=4), :].astype(BF16)
            return
        zs2 = zs2_ref.at[slot]
        for r in range(4):
            zs2[r * per:(r + 1) * per, :] = zs[pl.ds(r, per, stride=4), :]
        per2 = per // 4
        for r in range(4):
            for rp in range(4):
                r16 = r + 4 * rp
                o_ref[r16 * per2:(r16 + 1) * per2, cols] = (
                    zs2[pl.ds(r * per + rp, per2, stride=4), :].astype(BF16))

    def norm_rope(t, a_tab, b_tab, post):
        ms = jnp.mean(t * t, axis=-1, keepdims=True)
        rinv = lax.rsqrt(ms + EPS) * post
        return (t * a_tab + pltpu.roll(t, ROT_SHIFT, 1) * b_tab) * rinv

    slot = 0
    for which, (o_ref, g_ref, pg_ref, post) in enumerate(
            ((q_ref, gq_ref, pgq_ref, Q_SCALE), (k_ref, gk_ref, pgk_ref, 1.0), (v_ref, None, None, None))):
        if g_ref is not None:
            a_tab = cos * g_ref[...]
            b_tab = sin * pg_ref[...]
        for gi, dil in enumerate(DILATIONS):
            if g_ref is not None:
                base = which * ATTN_WIDTH + gi * ATTN_OUT
                z = jnp.dot(h, wqk_ref[:, base:base + ATTN_OUT], preferred_element_type=F32)
            else:
                z = jnp.dot(h, w_ref[:, gi * ATTN_OUT:(gi + 1) * ATTN_OUT], preferred_element_type=F32)
            for hh in range(HEADS_PER_GROUP):
                t = z[:, hh * HEAD_DIM:(hh + 1) * HEAD_DIM]
                if g_ref is not None:
                    t = norm_rope(t, a_tab, b_tab, post)
                store_head(o_ref, gi * ATTN_OUT + hh * HEAD_DIM, t, dil, slot % zs_ref.shape[0])
                slot += 1
            if conv_chunks:
                _dwconv31_chunk(conv_chunks.pop(0), ext_s, sh_s, y_s, wdw_ref, bdw_ref, tm)

    y = y_s[...]
    mu = jnp.mean(y, axis=-1, keepdims=True)
    yc = y - mu
    var = jnp.mean(yc * yc, axis=-1, keepdims=True)
    yn = yc * lax.rsqrt(var + EPS) * gln_ref[...] + bln_ref[...]
    ub = (yn * _sigmoid(yn)).astype(BF16)
    base_a = base_gate + D_MODEL
    base_b = base_a + D_MODEL
    for cc in range(D_MODEL // cw):
        cols = slice(cc * cw, (cc + 1) * cw)
        y_b = jnp.dot(ub, wco_ref[:, cols], preferred_element_type=F32)
        zb = jnp.dot(h, w_ref[:, base_b + cc * cw:base_b + (cc + 1) * cw], preferred_element_type=F32)
        gyb_ref[:, cols] = (_sigmoid(zb) * y_b).astype(BF16)
        za = jnp.dot(h, w_ref[:, base_a + cc * cw:base_a + (cc + 1) * cw], preferred_element_type=F32)
        ga_ref[:, cols] = _sigmoid(za).astype(BF16)


def _in_proj(x, mod, g1, w_qk, w_rest, gq, gk, pgq, pgk, cos_tab, sin_tab, wdw, bdw, gln, bln, wco, layer):
    bsz, seq, d = x.shape
    tm = TM_IN
    n_slots = 2
    row = lambda b, i: (b, i, 0)
    par = lambda b, i: (layer, 0, 0)
    vec = pl.BlockSpec((None, 1, d), par)
    head = pl.BlockSpec((None, 1, HEAD_DIM), par)
    out_attn = jax.ShapeDtypeStruct((bsz, seq, ATTN_WIDTH), BF16)
    out_d = jax.ShapeDtypeStruct((bsz, seq, d), BF16)
    return pl.pallas_call(
        _in_proj_kernel,
        grid=(bsz, seq // tm),
        in_specs=[pl.BlockSpec((None, tm, d), row),
                  pl.BlockSpec((None, None, 6, d), lambda b, i: (layer, b, 0, 0)),
                  vec,
                  pl.BlockSpec((None, d, 2 * ATTN_WIDTH), par, pipeline_mode=pl.Buffered(1)),
                  pl.BlockSpec((None, d, IN_WIDTH - 2 * ATTN_WIDTH), par, pipeline_mode=pl.Buffered(1)),
                  head, head, head, head,
                  pl.BlockSpec((None, tm, HEAD_DIM), row),
                  pl.BlockSpec((None, tm, HEAD_DIM), row),
                  pl.BlockSpec((None, CONV_K, d), par),
                  vec, vec, vec,
                  pl.BlockSpec((None, d, d), par, pipeline_mode=pl.Buffered(1))],
        out_specs=[pl.BlockSpec((None, tm, ATTN_WIDTH), row)] * 3 + [pl.BlockSpec((None, tm, d), row)] * 2,
        out_shape=[out_attn] * 3 + [out_d] * 2,
        scratch_shapes=[pltpu.VMEM((n_slots, tm, HEAD_DIM), F32), pltpu.VMEM((n_slots, tm, HEAD_DIM), F32),
                        pltpu.VMEM((tm + CONV_HALO, d), F32), pltpu.VMEM((tm, d), F32),
                        pltpu.VMEM((2, 7, tm + CONV_HALO - 8, HEAD_DIM), F32),
                        pltpu.VMEM((CONV_HALO, d), F32)],
        compiler_params=_params(2),
        name="in_proj",
    )(x, mod, g1, w_qk, w_rest, gq, gk, pgq, pgk, cos_tab, sin_tab, wdw, bdw, gln, bln, wco)


def _band_block(q, kc, vc, bias):
    s = lax.dot_general(q, kc, (((1,), (1,)), ((), ())), preferred_element_type=F32) + bias
    m = jnp.max(s, axis=-1, keepdims=True)
    p = jnp.exp2(s - m).astype(BF16)
    ones = jnp.ones((2 * BLOCK, HEAD_DIM), BF16)
    ol = jnp.dot(p, jnp.concatenate([vc, ones], axis=1), preferred_element_type=F32)
    return ol[:, :HEAD_DIM], jnp.broadcast_to(m, (BLOCK, HEAD_DIM)), ol[:, HEAD_DIM:]


def _attn_kernel(q0_ref, q1_ref, q2_ref, k0_ref, k1_ref, k2_ref, v0_ref, v1_ref, v2_ref,
                 k0h_ref, k1h_ref, k2h_ref, v0h_ref, v1h_ref, v2h_ref, o_ref,
                 a0_s, m0_s, l0_s, a1_s, m1_s, l1_s, a2_s, m2_s, l2_s, as_s, ms_s, ls_s, bias_s):
    first = pl.program_id(1) == 0
    qi = lax.broadcasted_iota(jnp.int32, (BLOCK, 2 * BLOCK), 0)
    kj = lax.broadcasted_iota(jnp.int32, (BLOCK, 2 * BLOCK), 1)
    band = (kj >= qi) & (kj <= qi + BLOCK)
    bias_s[0] = jnp.where(band, 0.0, NEG)
    bias_s[1] = jnp.where(band & (kj >= jnp.where(first, BLOCK, 0)), 0.0, NEG)
    inner, edge = 0, 1

    nblk = TILE_ATTN // BLOCK
    tin = TM_IN
    cat = functools.partial(jnp.concatenate, axis=0)

    def put(refs, idx, vals):
        for ref, val in zip(refs, vals):
            ref[idx, :] = val

    def rows(ref, n):
        return ref[n * BLOCK:(n + 1) * BLOCK, :]

    g0 = (a0_s, m0_s, l0_s)
    for n in range(nblk):
        if n == 0:
            kc, vc, which = cat([k0h_ref[...], rows(k0_ref, 0)]), cat([v0h_ref[...], rows(v0_ref, 0)]), edge
        else:
            win = slice((n - 1) * BLOCK, (n + 1) * BLOCK)
            kc, vc, which = k0_ref[win, :], v0_ref[win, :], inner
        put(g0, slice(n * BLOCK, (n + 1) * BLOCK), _band_block(rows(q0_ref, n), kc, vc, bias_s[which]))

    g1 = (a1_s, m1_s, l1_s)
    per_tile = tin // BLOCK
    for r in range(4):
        for n in range(TILE_ATTN // tin):
            cur = n * per_tile + r
            if n == 0:
                kp, vp, which = rows(k1h_ref, r), rows(v1h_ref, r), edge
            else:
                kp, vp, which = rows(k1_ref, cur - per_tile), rows(v1_ref, cur - per_tile), inner
            res = _band_block(rows(q1_ref, cur), cat([kp, rows(k1_ref, cur)]), cat([vp, rows(v1_ref, cur)]),
                              bias_s[which])
            put(g1, pl.ds(n * tin + r, BLOCK, stride=4), res)

    piece = tin // 16
    gs = (as_s, ms_s, ls_s)
    for r16 in range(16):
        pieces = lambda ref: [ref[c * tin + r16 * piece:c * tin + (r16 + 1) * piece, :]
                              for c in range(TILE_ATTN // tin)]
        res = _band_block(cat(pieces(q2_ref)), cat(pieces(k2h_ref) + pieces(k2_ref)),
                          cat(pieces(v2h_ref) + pieces(v2_ref)), bias_s[edge])
        put(gs, pl.ds((r16 % 4) * (TILE_ATTN // 4) + r16 // 4, BLOCK, stride=4), res)
    quarter = TILE_ATTN // 4
    for r in range(4):
        for dst, src in zip((a2_s, m2_s, l2_s), gs):
            dst[pl.ds(r, quarter, stride=4), :] = src[r * quarter:(r + 1) * quarter, :]

    ch = 256
    for c in range(TILE_ATTN // ch):
        rs = slice(c * ch, (c + 1) * ch)
        ma, mb, mc = m0_s[rs, :], m1_s[rs, :], m2_s[rs, :]
        mx = jnp.maximum(jnp.maximum(ma, mb), mc)
        ea, eb, ec = jnp.exp2(ma - mx), jnp.exp2(mb - mx), jnp.exp2(mc - mx)
        den = ea * l0_s[rs, :] + eb * l1_s[rs, :] + ec * l2_s[rs, :]
        num = ea * a0_s[rs, :] + eb * a1_s[rs, :] + ec * a2_s[rs, :]
        o_ref[rs, :] = (num * (1.0 / den)).astype(BF16)


def _attention(q, k, v):
    bsz, seq, _ = q.shape
    t = TILE_ATTN
    hpg = HEADS_PER_GROUP

    def cur(g):
        return pl.BlockSpec((None, t, HEAD_DIM), lambda b, i, j: (b, i, g * hpg + j))

    def halo(g, rows):
        per = t // rows
        return pl.BlockSpec((None, rows, HEAD_DIM),
                            lambda b, i, j: (b, jnp.maximum(i * per - 1, 0), g * hpg + j))

    halos = [halo(0, BLOCK), halo(1, TM_IN), halo(2, t)]
    scratch = [pltpu.VMEM((t, HEAD_DIM), F32)] * 12 + [pltpu.VMEM((2, BLOCK, 2 * BLOCK), F32)]
    return pl.pallas_call(
        _attn_kernel,
        grid=(bsz, seq // t, hpg),
        in_specs=[cur(0), cur(1), cur(2)] * 3 + halos * 2,
        out_specs=pl.BlockSpec((None, t, HEAD_DIM), lambda b, i, j: (b, i, j)),
        out_shape=jax.ShapeDtypeStruct((bsz, seq, ATTN_OUT), BF16),
        scratch_shapes=scratch,
        compiler_params=_params(3),
        name="attention",
    )(q, q, q, k, k, k, v, v, v, k, k, k, v, v, v)


def _mix_ffn_kernel(attn_ref, ga_ref, gyb_ref, x_ref, mod_ref, wap_ref, wo_ref, g2_ref, win_ref, wdw_ref, bdw_ref,
                    wdn_ref, o_ref, ext_s, carry_s, sh_s, act_s):
    tm = x_ref.shape[0]
    y_a = jnp.dot(attn_ref[...], wap_ref[...], preferred_element_type=F32)
    merged = ga_ref[...].astype(F32) * y_a + gyb_ref[...].astype(F32)
    x = x_ref[...] + mod_ref[2:3, :] * jnp.dot(merged.astype(BF16), wo_ref[...], preferred_element_type=F32)

    h = _rms_mod(x, g2_ref[...], mod_ref[3:4, :], mod_ref[4:5, :]).astype(BF16)

    @pl.when(pl.program_id(1) == 0)
    def _():
        carry_s[...] = jnp.zeros_like(carry_s)

    cw = FFN_CHUNK
    for c0 in range(0, D_FF, cw):
        w = min(cw, D_FF - c0)
        cols = slice(c0, c0 + w)
        g = jnp.dot(h, win_ref[:, c0:c0 + w], preferred_element_type=F32)
        up = jnp.dot(h, win_ref[:, D_FF + c0:D_FF + c0 + w], preferred_element_type=F32)
        ext_s[0:FFN_HALO, 0:w] = carry_s[:, cols]
        ext_s[FFN_HALO:, 0:w] = g
        carry_s[:, cols] = g[tm - FFN_HALO:, :]
        gc = bdw_ref[:, cols] + wdw_ref[2:3, cols] * g
        for kk in range(FFN_K - 1):
            start = FFN_HALO - (FFN_K - 1) + kk
            sh_s[kk, :, 0:w] = ext_s[start:start + tm, 0:w]
            gc = gc + wdw_ref[kk:kk + 1, cols] * sh_s[kk, :, 0:w]
        act_s[:, cols] = (gc * _sigmoid(gc) * up).astype(BF16)
    out = jnp.dot(act_s[...], wdn_ref[...], preferred_element_type=F32)
    o_ref[...] = x + mod_ref[5:6, :] * out


def _mix_ffn(attn, ga, gyb, x, mod, wap, wo, g2, win, wdw, bdw, wdn, layer):
    bsz, seq, d = x.shape
    tm = TM_FFN
    row = lambda b, i: (b, i, 0)
    par = lambda b, i: (layer, 0, 0)
    once = dict(pipeline_mode=pl.Buffered(1))
    return pl.pallas_call(
        _mix_ffn_kernel,
        grid=(bsz, seq // tm),
        in_specs=[pl.BlockSpec((None, tm, ATTN_OUT), row),
                  pl.BlockSpec((None, tm, d), row),
                  pl.BlockSpec((None, tm, d), row),
                  pl.BlockSpec((None, tm, d), row),
                  pl.BlockSpec((None, None, 6, d), lambda b, i: (layer, b, 0, 0)),
                  pl.BlockSpec((None, ATTN_OUT, d), par, **once),
                  pl.BlockSpec((None, d, d), par, **once),
                  pl.BlockSpec((None, 1, d), par),
                  pl.BlockSpec((None, d, 2 * D_FF), par, **once),
                  pl.BlockSpec((None, FFN_K, D_FF), par),
                  pl.BlockSpec((None, 1, D_FF), par),
                  pl.BlockSpec((None, D_FF, d), par, **once)],
        out_specs=pl.BlockSpec((None, tm, d), row),
        out_shape=jax.ShapeDtypeStruct((bsz, seq, d), F32),
        scratch_shapes=[pltpu.VMEM((tm + FFN_HALO, FFN_CHUNK), F32), pltpu.VMEM((FFN_HALO, D_FF), F32),
                        pltpu.VMEM((FFN_K - 1, tm, FFN_CHUNK), F32), pltpu.VMEM((tm, D_FF), BF16)],
        compiler_params=_params(2),
        name="mix_ffn",
    )(attn, ga, gyb, x, mod, wap, wo, g2, win, wdw, bdw, wdn)


def _rope_tables(positions):
    inv_freq = ROPE_THETA ** (-jnp.arange(0, ROT_DIM, 2, dtype=F32) / ROT_DIM)
    ang = positions.astype(F32)[..., None] * inv_freq
    lane = jnp.arange(HEAD_DIM)[None, :]
    freq = jnp.arange(ROT_HALF)[:, None]
    first, second = lane == freq, lane == freq + ROT_SHIFT
    spread_cos = (first | second).astype(F32)
    spread_sin = second.astype(F32) - first.astype(F32)
    plain = 1.0 - jnp.sum(spread_cos, axis=0)
    spread = functools.partial(jnp.einsum, "bsf,fl->bsl", precision=lax.Precision.HIGHEST)
    return spread(jnp.cos(ang), spread_cos) + plain, spread(jnp.sin(ang), spread_sin)


def kernel(x, c, positions, w_ada, b_ada, g_norm1, w_in, g_q, g_k, w_attn_proj, w_conv_dw, b_conv_dw,
           g_conv_ln, b_conv_ln, w_conv_out, w_o, g_norm2, w_ffn_in, w_ffn_dw, b_ffn_dw, w_ffn_down):
    depth = w_ada.shape[0]
    bsz, seq, d = x.shape
    assert (d, seq % TILE_ATTN) == (D_MODEL, 0)

    mod = _ada(c, w_ada, b_ada).reshape(depth, bsz, 6, d)
    cos_tab, sin_tab = _rope_tables(positions)
    vec = lambda a: a.reshape(depth, 1, a.shape[-1])
    perm = jnp.asarray(HEAD_PERM)
    perm_mat = jnp.eye(HEAD_DIM, dtype=F32)[:, perm]
    w_qk = jnp.dot(w_in[:, :, :2 * ATTN_WIDTH].reshape(-1, HEAD_DIM), perm_mat,
                   preferred_element_type=BF16).reshape(depth, d, 2 * ATTN_WIDTH)
    w_rest = w_in[:, :, 2 * ATTN_WIDTH:].astype(BF16)
    g_q, g_k = g_q[:, perm], g_k[:, perm]
    wap_b, wco_b, wo_b = (w.astype(BF16) for w in (w_attn_proj, w_conv_out, w_o))
    win_b, wdn_b = w_ffn_in.astype(BF16), w_ffn_down.astype(BF16)
    g1, g2, gq, gk = vec(g_norm1), vec(g_norm2), vec(g_q), vec(g_k)
    pgq, pgk = vec(jnp.roll(g_q, ROT_SHIFT, axis=-1)), vec(jnp.roll(g_k, ROT_SHIFT, axis=-1))
    bdw, gln, bln, bfdw = vec(b_conv_dw), vec(g_conv_ln), vec(b_conv_ln), vec(b_ffn_dw)

    for layer in range(depth):
        q, k, v, ga, gyb = _in_proj(x, mod, g1, w_qk, w_rest, gq, gk, pgq, pgk, cos_tab, sin_tab,
                                    w_conv_dw, bdw, gln, bln, wco_b, layer)
        attn = _attention(q, k, v)
        x = _mix_ffn(attn, ga, gyb, x, mod, wap_b, wo_b, g2, win_b, w_ffn_dw, bfdw, wdn_b, layer)
    return x
```

```python
import functools

import jax
import jax.numpy as jnp
from jax import lax
from jax.experimental import pallas as pl
from jax.experimental.pallas import tpu as pltpu

F32 = jnp.float32
BF16 = jnp.bfloat16

D_MODEL = 1024
HEAD_DIM = 128
DILATIONS = (1, 4, 16)
HEADS_PER_GROUP = 4
N_HEADS = HEADS_PER_GROUP * len(DILATIONS)
ATTN_WIDTH = N_HEADS * HEAD_DIM
ATTN_OUT = HEADS_PER_GROUP * HEAD_DIM
BLOCK = 128
ROT_DIM = HEAD_DIM // 4
ROT_HALF = ROT_DIM // 2
ROT_SHIFT = HEAD_DIM // 2
HEAD_PERM = (tuple(range(ROT_HALF)) + tuple(range(ROT_DIM, ROT_SHIFT + ROT_HALF))
             + tuple(range(ROT_HALF, ROT_DIM)) + tuple(range(ROT_SHIFT + ROT_HALF, HEAD_DIM)))
ROPE_THETA = 500000.0
CONV_K = 31
D_FF = 2816
FFN_K = 3
IN_WIDTH = 3 * ATTN_WIDTH + 4 * D_MODEL
EPS = 1e-6
NEG = -1e30
Q_SCALE = HEAD_DIM ** -0.5 * 1.4426950408889634

TM_IN = 512
TILE_ATTN = 2048
TM_FFN = 512
CONV_HALO = 32
FFN_HALO = 8
FFN_CHUNK = 768
V7X_VMEM_LIMIT = 56 * 1024 * 1024


def _params(n_axes):
    return pltpu.CompilerParams(dimension_semantics=("arbitrary",) * n_axes,
                                vmem_limit_bytes=V7X_VMEM_LIMIT)


def _sigmoid(x):
    return 1.0 / (1.0 + jnp.exp(-x))


def _ada_kernel(c_ref, w_ref, b_ref, o_ref):
    c = c_ref[...]
    c_act = c * _sigmoid(c)
    o_ref[...] = jnp.dot(c_act, w_ref[...], preferred_element_type=F32) + b_ref[...]


def _ada(c, w_ada, b_ada):
    depth, d, n = w_ada.shape
    bsz = c.shape[0]
    nt = 1024
    return pl.pallas_call(
        _ada_kernel,
        grid=(depth, n // nt),
        in_specs=[pl.BlockSpec((bsz, d), lambda l, j: (0, 0)),
                  pl.BlockSpec((None, d, nt), lambda l, j: (l, 0, j)),
                  pl.BlockSpec((None, 1, nt), lambda l, j: (l, 0, j))],
        out_specs=pl.BlockSpec((None, bsz, nt), lambda l, j: (l, 0, j)),
        out_shape=jax.ShapeDtypeStruct((depth, bsz, n), F32),
        compiler_params=_params(2),
        name="ada",
    )(c, w_ada, b_ada.reshape(depth, 1, n))


def _rms_mod(x, g, shift, scale):
    ms = jnp.mean(x * x, axis=-1, keepdims=True)
    h = x * lax.rsqrt(ms + EPS) * g
    return h * (1.0 + scale) + shift


def _dwconv31_chunk(c, ext_s, sh_s, y_s, wdw_ref, bdw_ref, tm):
    rc = 64
    lead = CONV_HALO - (CONV_K - 1)
    rows_sh = sh_s.shape[2]
    cols = slice(c * HEAD_DIM, (c + 1) * HEAD_DIM)
    sh = sh_s.at[c % sh_s.shape[0]]
    for s in range(1, 8):
        sh[s - 1, :, :] = ext_s[s:s + rows_sh, cols]
    taps = [jnp.broadcast_to(wdw_ref[kk:kk + 1, cols], (rc, HEAD_DIM)) for kk in range(CONV_K)]
    bias = jnp.broadcast_to(bdw_ref[:, cols], (rc, HEAD_DIM))
    for i in range(tm // rc):
        acc = bias
        for kk in range(CONV_K):
            r0 = i * rc + (lead + kk) // 8 * 8
            s = (lead + kk) % 8
            win = ext_s[r0:r0 + rc, cols] if s == 0 else sh[s - 1, r0:r0 + rc, :]
            acc = acc + taps[kk] * win
        y_s[i * rc:(i + 1) * rc, cols] = acc


def _in_proj_kernel(x_ref, mod_ref, g1_ref, wqk_ref, w_ref, gq_ref, gk_ref, pgq_ref, pgk_ref, cos_ref, sin_ref,
                    wdw_ref, bdw_ref, gln_ref, bln_ref, wco_ref,
                    q_ref, k_ref, v_ref, ga_ref, gyb_ref,
                    zs_ref, zs2_ref, ext_s, y_s, sh_s, carry_s):
    tm = x_ref.shape[0]
    h = _rms_mod(x_ref[...], g1_ref[...], mod_ref[0:1, :], mod_ref[1:2, :]).astype(BF16)

    @pl.when(pl.program_id(1) == 0)
    def _():
        carry_s[...] = jnp.zeros_like(carry_s)

    cw = 512
    base_val = ATTN_WIDTH
    base_gate = base_val + D_MODEL
    ext_s[0:CONV_HALO, :] = carry_s[...]
    for cc in range(D_MODEL // cw):
        val = jnp.dot(h, w_ref[:, base_val + cc * cw:base_val + (cc + 1) * cw], preferred_element_type=F32)
        gate = jnp.dot(h, w_ref[:, base_gate + cc * cw:base_gate + (cc + 1) * cw], preferred_element_type=F32)
        ext_s[CONV_HALO:, cc * cw:(cc + 1) * cw] = val * _sigmoid(gate)
    carry_s[...] = ext_s[tm:tm + CONV_HALO, :]
    conv_chunks = list(range(D_MODEL // HEAD_DIM))

    cos = cos_ref[...]
    sin = sin_ref[...]

    def store_head(o_ref, col, val, dil, slot):
        cols = slice(col, col + HEAD_DIM)
        if dil == 1:
            o_ref[:, cols] = val.astype(BF16)
            return
        zs = zs_ref.at[slot]
        zs[...] = val
        per = tm // 4
        if dil == 4:
            for r in range(4):
                o_ref[r * per:(r + 1) * per, cols] = zs[pl.ds(r, per, stride=4), :].astype(BF16)
            return
        zs2 = zs2_ref.at[slot]
        for r in range(4):
            zs2[r * per:(r + 1) * per, :] = zs[pl.ds(r, per, stride=4), :]
        per2 = per // 4
        for r in range(4):
            for rp in range(4):
                r16 = r + 4 * rp
                o_ref[r16 * per2:(r16 + 1) * per2, cols] = (
                    zs2[pl.ds(r * per + rp, per2, stride=4), :].astype(BF16))

    def norm_rope(t, a_tab, b_tab, post):
        ms = jnp.mean(t * t, axis=-1, keepdims=True)
        rinv = lax.rsqrt(ms + EPS) * post
        return (t * a_tab + pltpu.roll(t, ROT_SHIFT, 1) * b_tab) * rinv

    slot = 0
    for which, (o_ref, g_ref, pg_ref, post) in enumerate(
            ((q_ref, gq_ref, pgq_ref, Q_SCALE), (k_ref, gk_ref, pgk_ref, 1.0), (v_ref, None, None, None))):
        if g_ref is not None:
            a_tab = cos * g_ref[...]
            b_tab = sin * pg_ref[...]
        for gi, dil in enumerate(DILATIONS):
            if g_ref is not None:
                base = which * ATTN_WIDTH + gi * ATTN_OUT
                z = jnp.dot(h, wqk_ref[:, base:base + ATTN_OUT], preferred_element_type=F32)
            else:
                z = jnp.dot(h, w_ref[:, gi * ATTN_OUT:(gi + 1) * ATTN_OUT], preferred_element_type=F32)
            for hh in range(HEADS_PER_GROUP):
                t = z[:, hh * HEAD_DIM:(hh + 1) * HEAD_DIM]
                if g_ref is not None:
                    t = norm_rope(t, a_tab, b_tab, post)
                store_head(o_ref, gi * ATTN_OUT + hh * HEAD_DIM, t, dil, slot % zs_ref.shape[0])
                slot += 1
            if conv_chunks:
                _dwconv31_chunk(conv_chunks.pop(0), ext_s, sh_s, y_s, wdw_ref, bdw_ref, tm)

    y = y_s[...]
    mu = jnp.mean(y, axis=-1, keepdims=True)
    yc = y - mu
    var = jnp.mean(yc * yc, axis=-1, keepdims=True)
    yn = yc * lax.rsqrt(var + EPS) * gln_ref[...] + bln_ref[...]
    ub = (yn * _sigmoid(yn)).astype(BF16)
    base_a = base_gate + D_MODEL
    base_b = base_a + D_MODEL
    for cc in range(D_MODEL // cw):
        cols = slice(cc * cw, (cc + 1) * cw)
        y_b = jnp.dot(ub, wco_ref[:, cols], preferred_element_type=F32)
        zb = jnp.dot(h, w_ref[:, base_b + cc * cw:base_b + (cc + 1) * cw], preferred_element_type=F32)
        gyb_ref[:, cols] = (_sigmoid(zb) * y_b).astype(BF16)
        za = jnp.dot(h, w_ref[:, base_a + cc * cw:base_a + (cc + 1) * cw], preferred_element_type=F32)
        ga_ref[:, cols] = _sigmoid(za).astype(BF16)


def _in_proj(x, mod, g1, w_qk, w_rest, gq, gk, pgq, pgk, cos_tab, sin_tab, wdw, bdw, gln, bln, wco, layer):
    bsz, seq, d = x.shape
    tm = TM_IN
    n_slots = 2
    row = lambda b, i: (b, i, 0)
    par = lambda b, i: (layer, 0, 0)
    vec = pl.BlockSpec((None, 1, d), par)
    head = pl.BlockSpec((None, 1, HEAD_DIM), par)
    out_attn = jax.ShapeDtypeStruct((bsz, seq, ATTN_WIDTH), BF16)
    out_d = jax.ShapeDtypeStruct((bsz, seq, d), BF16)
    return pl.pallas_call(
        _in_proj_kernel,
        grid=(bsz, seq // tm),
        in_specs=[pl.BlockSpec((None, tm, d), row),
                  pl.BlockSpec((None, None, 6, d), lambda b, i: (layer, b, 0, 0)),
                  vec,
                  pl.BlockSpec((None, d, 2 * ATTN_WIDTH), par, pipeline_mode=pl.Buffered(1)),
                  pl.BlockSpec((None, d, IN_WIDTH - 2 * ATTN_WIDTH), par, pipeline_mode=pl.Buffered(1)),
                  head, head, head, head,
                  pl.BlockSpec((None, tm, HEAD_DIM), row),
                  pl.BlockSpec((None, tm, HEAD_DIM), row),
                  pl.BlockSpec((None, CONV_K, d), par),
                  vec, vec, vec,
                  pl.BlockSpec((None, d, d), par, pipeline_mode=pl.Buffered(1))],
        out_specs=[pl.BlockSpec((None, tm, ATTN_WIDTH), row)] * 3 + [pl.BlockSpec((None, tm, d), row)] * 2,
        out_shape=[out_attn] * 3 + [out_d] * 2,
        scratch_shapes=[pltpu.VMEM((n_slots, tm, HEAD_DIM), F32), pltpu.VMEM((n_slots, tm, HEAD_DIM), F32),
                        pltpu.VMEM((tm + CONV_HALO, d), F32), pltpu.VMEM((tm, d), F32),
                        pltpu.VMEM((2, 7, tm + CONV_HALO - 8, HEAD_DIM), F32),
                        pltpu.VMEM((CONV_HALO, d), F32)],
        compiler_params=_params(2),
        name="in_proj",
    )(x, mod, g1, w_qk, w_rest, gq, gk, pgq, pgk, cos_tab, sin_tab, wdw, bdw, gln, bln, wco)


def _band_block(q, kc, vc, bias):
    s = lax.dot_general(q, kc, (((1,), (1,)), ((), ())), preferred_element_type=F32) + bias
    m = jnp.max(s, axis=-1, keepdims=True)
    p = jnp.exp2(s - m).astype(BF16)
    ones = jnp.ones((2 * BLOCK, HEAD_DIM), BF16)
    ol = jnp.dot(p, jnp.concatenate([vc, ones], axis=1), preferred_element_type=F32)
    return ol[:, :HEAD_DIM], jnp.broadcast_to(m, (BLOCK, HEAD_DIM)), ol[:, HEAD_DIM:]


def _attn_kernel(q0_ref, q1_ref, q2_ref, k0_ref, k1_ref, k2_ref, v0_ref, v1_ref, v2_ref,
                 k0h_ref, k1h_ref, k2h_ref, v0h_ref, v1h_ref, v2h_ref, o_ref,
                 a0_s, m0_s, l0_s, a1_s, m1_s, l1_s, a2_s, m2_s, l2_s, as_s, ms_s, ls_s, bias_s):
    first = pl.program_id(1) == 0
    qi = lax.broadcasted_iota(jnp.int32, (BLOCK, 2 * BLOCK), 0)
    kj = lax.broadcasted_iota(jnp.int32, (BLOCK, 2 * BLOCK), 1)
    band = (kj >= qi) & (kj <= qi + BLOCK)
    bias_s[0] = jnp.where(band, 0.0, NEG)
    bias_s[1] = jnp.where(band & (kj >= jnp.where(first, BLOCK, 0)), 0.0, NEG)
    inner, edge = 0, 1

    nblk = TILE_ATTN // BLOCK
    tin = TM_IN
    cat = functools.partial(jnp.concatenate, axis=0)

    def put(refs, idx, vals):
        for ref, val in zip(refs, vals):
            ref[idx, :] = val

    def rows(ref, n):
        return ref[n * BLOCK:(n + 1) * BLOCK, :]

    g0 = (a0_s, m0_s, l0_s)
    for n in range(nblk):
        if n == 0:
            kc, vc, which = cat([k0h_ref[...], rows(k0_ref, 0)]), cat([v0h_ref[...], rows(v0_ref, 0)]), edge
        else:
            win = slice((n - 1) * BLOCK, (n + 1) * BLOCK)
            kc, vc, which = k0_ref[win, :], v0_ref[win, :], inner
        put(g0, slice(n * BLOCK, (n + 1) * BLOCK), _band_block(rows(q0_ref, n), kc, vc, bias_s[which]))

    g1 = (a1_s, m1_s, l1_s)
    per_tile = tin // BLOCK
    for r in range(4):
        for n in range(TILE_ATTN // tin):
            cur = n * per_tile + r
            if n == 0:
                kp, vp, which = rows(k1h_ref, r), rows(v1h_ref, r), edge
            else:
                kp, vp, which = rows(k1_ref, cur - per_tile), rows(v1_ref, cur - per_tile), inner
            res = _band_block(rows(q1_ref, cur), cat([kp, rows(k1_ref, cur)]), cat([vp, rows(v1_ref, cur)]),
                              bias_s[which])
            put(g1, pl.ds(n * tin + r, BLOCK, stride=4), res)

    piece = tin // 16
    gs = (as_s, ms_s, ls_s)
    for r16 in range(16):
        pieces = lambda ref: [ref[c * tin + r16 * piece:c * tin + (r16 + 1) * piece, :]
                              for c in range(TILE_ATTN // tin)]
        res = _band_block(cat(pieces(q2_ref)), cat(pieces(k2h_ref) + pieces(k2_ref)),
                          cat(pieces(v2h_ref) + pieces(v2_ref)), bias_s[edge])
        put(gs, pl.ds((r16 % 4) * (TILE_ATTN // 4) + r16 // 4, BLOCK, stride=4), res)
    quarter = TILE_ATTN // 4
    for r in range(4):
        for dst, src in zip((a2_s, m2_s, l2_s), gs):
            dst[pl.ds(r, quarter, stride=4), :] = src[r * quarter:(r + 1) * quarter, :]

    ch = 256
    for c in range(TILE_ATTN // ch):
        rs = slice(c * ch, (c + 1) * ch)
        ma, mb, mc = m0_s[rs, :], m1_s[rs, :], m2_s[rs, :]
        mx = jnp.maximum(jnp.maximum(ma, mb), mc)
        ea, eb, ec = jnp.exp2(ma - mx), jnp.exp2(mb - mx), jnp.exp2(mc - mx)
        den = ea * l0_s[rs, :] + eb * l1_s[rs, :] + ec * l2_s[rs, :]
        num = ea * a0_s[rs, :] + eb * a1_s[rs, :] + ec * a2_s[rs, :]
        o_ref[rs, :] = (num * (1.0 / den)).astype(BF16)


def _attention(q, k, v):
    bsz, seq, _ = q.shape
    t = TILE_ATTN
    hpg = HEADS_PER_GROUP

    def cur(g):
        return pl.BlockSpec((None, t, HEAD_DIM), lambda b, i, j: (b, i, g * hpg + j))

    def halo(g, rows):
        per = t // rows
        return pl.BlockSpec((None, rows, HEAD_DIM),
                            lambda b, i, j: (b, jnp.maximum(i * per - 1, 0), g * hpg + j))

    halos = [halo(0, BLOCK), halo(1, TM_IN), halo(2, t)]
    scratch = [pltpu.VMEM((t, HEAD_DIM), F32)] * 12 + [pltpu.VMEM((2, BLOCK, 2 * BLOCK), F32)]
    return pl.pallas_call(
        _attn_kernel,
        grid=(bsz, seq // t, hpg),
        in_specs=[cur(0), cur(1), cur(2)] * 3 + halos * 2,
        out_specs=pl.BlockSpec((None, t, HEAD_DIM), lambda b, i, j: (b, i, j)),
        out_shape=jax.ShapeDtypeStruct((bsz, seq, ATTN_OUT), BF16),
        scratch_shapes=scratch,
        compiler_params=_params(3),
        name="attention",
    )(q, q, q, k, k, k, v, v, v, k, k, k, v, v, v)


def _mix_ffn_kernel(attn_ref, ga_ref, gyb_ref, x_ref, mod_ref, wap_ref, wo_ref, g2_ref, win_ref, wdw_ref, bdw_ref,
                    wdn_ref, o_ref, ext_s, carry_s, sh_s, act_s):
    tm = x_ref.shape[0]
    y_a = jnp.dot(attn_ref[...], wap_ref[...], preferred_element_type=F32)
    merged = ga_ref[...].astype(F32) * y_a + gyb_ref[...].astype(F32)
    x = x_ref[...] + mod_ref[2:3, :] * jnp.dot(merged.astype(BF16), wo_ref[...], preferred_element_type=F32)

    h = _rms_mod(x, g2_ref[...], mod_ref[3:4, :], mod_ref[4:5, :]).astype(BF16)

    @pl.when(pl.program_id(1) == 0)
    def _():
        carry_s[...] = jnp.zeros_like(carry_s)

    cw = FFN_CHUNK
    for c0 in range(0, D_FF, cw):
        w = min(cw, D_FF - c0)
        cols = slice(c0, c0 + w)
        g = jnp.dot(h, win_ref[:, c0:c0 + w], preferred_element_type=F32)
        up = jnp.dot(h, win_ref[:, D_FF + c0:D_FF + c0 + w], preferred_element_type=F32)
        ext_s[0:FFN_HALO, 0:w] = carry_s[:, cols]
        ext_s[FFN_HALO:, 0:w] = g
        carry_s[:, cols] = g[tm - FFN_HALO:, :]
        gc = bdw_ref[:, cols] + wdw_ref[2:3, cols] * g
        for kk in range(FFN_K - 1):
            start = FFN_HALO - (FFN_K - 1) + kk
            sh_s[kk, :, 0:w] = ext_s[start:start + tm, 0:w]
            gc = gc + wdw_ref[kk:kk + 1, cols] * sh_s[kk, :, 0:w]
        act_s[:, cols] = (gc * _sigmoid(gc) * up).astype(BF16)
    out = jnp.dot(act_s[...], wdn_ref[...], preferred_element_type=F32)
    o_ref[...] = x + mod_ref[5:6, :] * out


def _mix_ffn(attn, ga, gyb, x, mod, wap, wo, g2, win, wdw, bdw, wdn, layer):
    bsz, seq, d = x.shape
    tm = TM_FFN
    row = lambda b, i: (b, i, 0)
    par = lambda b, i: (layer, 0, 0)
    once = dict(pipeline_mode=pl.Buffered(1))
    return pl.pallas_call(
        _mix_ffn_kernel,
        grid=(bsz, seq // tm),
        in_specs=[pl.BlockSpec((None, tm, ATTN_OUT), row),
                  pl.BlockSpec((None, tm, d), row),
                  pl.BlockSpec((None, tm, d), row),
                  pl.BlockSpec((None, tm, d), row),
                  pl.BlockSpec((None, None, 6, d), lambda b, i: (layer, b, 0, 0)),
                  pl.BlockSpec((None, ATTN_OUT, d), par, **once),
                  pl.BlockSpec((None, d, d), par, **once),
                  pl.BlockSpec((None, 1, d), par),
                  pl.BlockSpec((None, d, 2 * D_FF), par, **once),
                  pl.BlockSpec((None, FFN_K, D_FF), par),
                  pl.BlockSpec((None, 1, D_FF), par),
                  pl.BlockSpec((None, D_FF, d), par, **once)],
        out_specs=pl.BlockSpec((None, tm, d), row),
        out_shape=jax.ShapeDtypeStruct((bsz, seq, d), F32),
        scratch_shapes=[pltpu.VMEM((tm + FFN_HALO, FFN_CHUNK), F32), pltpu.VMEM((FFN_HALO, D_FF), F32),
                        pltpu.VMEM((FFN_K - 1, tm, FFN_CHUNK), F32), pltpu.VMEM((tm, D_FF), BF16)],
        compiler_params=_params(2),
        name="mix_ffn",
    )(attn, ga, gyb, x, mod, wap, wo, g2, win, wdw, bdw, wdn)


def _rope_tables(positions):
    group = HEAD_DIM // ROT_HALF
    inv_freq = ROPE_THETA ** (-jnp.arange(0, ROT_DIM, 2, dtype=F32) / ROT_DIM)
    pos = positions.astype(F32).reshape(-1, group, 1)
    ang = (pos * inv_freq).reshape(-1, HEAD_DIM)
    lane = jnp.arange(HEAD_DIM)[None, :]
    freq = jnp.arange(ROT_HALF)[:, None]
    first, second = lane == freq, lane == freq + ROT_SHIFT
    spread_cos = (first | second).astype(F32)
    spread_sin = second.astype(F32) - first.astype(F32)
    plain = jnp.tile(1.0 - jnp.sum(spread_cos, axis=0), group)
    same_token = jnp.eye(group, dtype=F32)

    def spread(vals, mat):
        per_group = jnp.einsum("tu,fl->tful", same_token, mat).reshape(HEAD_DIM, group * HEAD_DIM)
        return jnp.dot(vals, per_group, precision=lax.Precision.HIGHEST)

    shape = positions.shape + (HEAD_DIM,)
    return (spread(jnp.cos(ang), spread_cos) + plain).reshape(shape), spread(jnp.sin(ang), spread_sin).reshape(shape)


def kernel(x, c, positions, w_ada, b_ada, g_norm1, w_in, g_q, g_k, w_attn_proj, w_conv_dw, b_conv_dw,
           g_conv_ln, b_conv_ln, w_conv_out, w_o, g_norm2, w_ffn_in, w_ffn_dw, b_ffn_dw, w_ffn_down):
    depth = w_ada.shape[0]
    bsz, seq, d = x.shape
    assert (d, seq % TILE_ATTN) == (D_MODEL, 0)

    mod = _ada(c, w_ada, b_ada).reshape(depth, bsz, 6, d)
    cos_tab, sin_tab = _rope_tables(positions)
    vec = lambda a: a.reshape(depth, 1, a.shape[-1])
    perm = jnp.asarray(HEAD_PERM)
    perm_mat = jnp.eye(HEAD_DIM, dtype=F32)[:, perm]
    w_qk = jnp.dot(w_in[:, :, :2 * ATTN_WIDTH].reshape(-1, HEAD_DIM), perm_mat,
                   preferred_element_type=BF16).reshape(depth, d, 2 * ATTN_WIDTH)
    w_rest = w_in[:, :, 2 * ATTN_WIDTH:].astype(BF16)
    g_q, g_k = g_q[:, perm], g_k[:, perm]
    wap_b, wco_b, wo_b = (w.astype(BF16) for w in (w_attn_proj, w_conv_out, w_o))
    win_b, wdn_b = w_ffn_in.astype(BF16), w_ffn_down.astype(BF16)
    g1, g2, gq, gk = vec(g_norm1), vec(g_norm2), vec(g_q), vec(g_k)
    pgq, pgk = vec(jnp.roll(g_q, ROT_SHIFT, axis=-1)), vec(jnp.roll(g_k, ROT_SHIFT, axis=-1))
    bdw, gln, bln, bfdw = vec(b_conv_dw), vec(g_conv_ln), vec(b_conv_ln), vec(b_ffn_dw)

    for layer in range(depth):
        q, k, v, ga, gyb = _in_proj(x, mod, g1, w_qk, w_rest, gq, gk, pgq, pgk, cos_tab, sin_tab,
                                    w_conv_dw, bdw, gln, bln, wco_b, layer)
        attn = _attention(q, k, v)
        x = _mix_ffn(attn, ga, gyb, x, mod, wap_b, wo_b, g2, win_b, w_ffn_dw, bfdw, wdn_b, layer)
    return x
```
